```python
import jax, jax.numpy as jnp
from jax import lax
import numpy as np

D_MODEL = 4096
BATCH = 2
SEQ = 8192
DEPTH = 2

A_HEADS = 16
A_HEAD_DIM = 128
A_WIDTH = A_HEADS * A_HEAD_DIM
KV_RANK = 512
IDX_HEADS = 32
IDX_DIM = 128
TOPK_MAX = 256
Q_BLOCK = 128
B_GROUPS = 16
B_GROUP_DIM = 128
B_WIDTH = B_GROUPS * B_GROUP_DIM
CHUNK = 128
MIX_WIDTH = A_WIDTH + B_WIDTH
SPLITS = (A_WIDTH, KV_RANK, A_WIDTH, IDX_HEADS * IDX_DIM, IDX_DIM, IDX_HEADS, B_WIDTH, B_WIDTH, B_WIDTH)
N_IN = A_WIDTH + KV_RANK + A_WIDTH + IDX_HEADS * IDX_DIM + IDX_DIM + IDX_HEADS + 3 * B_WIDTH
EPS = 1e-6

kernel_name = "hymba_dsa_gmlp_hybrid"


def _offsets():
    offs, acc = [], 0
    for s in SPLITS[:-1]:
        acc += s
        offs.append(acc)
    return offs


def rmsnorm(x, g):
    xf = x.astype(jnp.float32)
    y = xf * lax.rsqrt(jnp.mean(xf * xf, axis=-1, keepdims=True) + EPS)
    return (y * g.astype(jnp.float32)).astype(x.dtype)


def layernorm(x, g, b):
    xf = x.astype(jnp.float32)
    mu = jnp.mean(xf, axis=-1, keepdims=True)
    xc = xf - mu
    y = xc * lax.rsqrt(jnp.mean(xc * xc, axis=-1, keepdims=True) + EPS)
    return (y * g.astype(jnp.float32) + b.astype(jnp.float32)).astype(x.dtype)


def dsa_attention(q, c_kv, q_idx, k_idx, w_idx, w_uk, w_uv):
    bsz, L = q.shape[0], q.shape[1]
    topk = min(TOPK_MAX, L // 4)
    nblk = L // Q_BLOCK
    neg = jnp.finfo(jnp.float32).min
    q_lat = jnp.einsum('bshd,chd->bshc', q, w_uk)
    key_pos = jnp.arange(L)

    def to_blocks(t):
        return jnp.moveaxis(t.reshape((bsz, nblk, Q_BLOCK) + t.shape[2:]), 1, 0)

    def block(args):
        ql, qi, wi, start = args
        t_pos = start + jnp.arange(Q_BLOCK)
        causal = key_pos[None, :] <= t_pos[:, None]
        logits = jnp.einsum('bqhd,bsd->bqhs', qi, k_idx).astype(jnp.float32) * (IDX_DIM ** -0.5)
        score = jnp.einsum('bqhs,bqh->bqs', jax.nn.relu(logits), wi.astype(jnp.float32))
        score = jnp.where(causal[None], score, neg)
        _, idx = lax.top_k(score, topk)
        valid = idx <= t_pos[None, :, None]
        c_sel = jax.vmap(lambda c, i: c[i])(c_kv, idx)
        s = jnp.einsum('bqhc,bqkc->bqhk', ql, c_sel).astype(jnp.float32) * (A_HEAD_DIM ** -0.5)
        s = jnp.where(valid[:, :, None, :], s, -jnp.inf)
        p = jax.nn.softmax(s, axis=-1).astype(c_sel.dtype)
        o_lat = jnp.einsum('bqhk,bqkc->bqhc', p, c_sel)
        return jnp.einsum('bqhc,chd->bqhd', o_lat, w_uv)

    starts = jnp.arange(nblk) * Q_BLOCK
    out = lax.map(block, (to_blocks(q_lat), to_blocks(q_idx), to_blocks(w_idx), starts))
    return jnp.moveaxis(out, 0, 1).reshape(bsz, L, A_HEADS, A_HEAD_DIM)


def spatial_gating(u, v, w_s, b_s):
    bsz, L = u.shape[0], u.shape[1]
    nc = L // CHUNK
    mask = jnp.tril(jnp.ones((CHUNK, CHUNK), dtype=bool))
    ws = jnp.where(mask[None], w_s, jnp.zeros_like(w_s))
    vc = v.reshape(bsz, nc, CHUNK, B_GROUPS, B_GROUP_DIM)
    mixed = jnp.einsum('gij,bcjge->bcige', ws, vc) + jnp.transpose(b_s)[None, None, :, :, None]
    return u * mixed.reshape(u.shape)


def setup_inputs(seed: int = 0) -> dict:
    key = jax.random.key(seed)
    ks = jax.random.split(key, 16)
    f32 = jnp.float32
    x = jax.random.normal(ks[0], (BATCH, SEQ, D_MODEL), f32)
    norm_g = 1.0 + 0.02 * jax.random.normal(ks[1], (DEPTH, D_MODEL), f32)
    w_in = jax.random.normal(ks[2], (DEPTH, D_MODEL, N_IN), f32) * (D_MODEL ** -0.5)
    kv_norm_g = 1.0 + 0.02 * jax.random.normal(ks[3], (DEPTH, KV_RANK), f32)
    idx_k_norm_g = 1.0 + 0.02 * jax.random.normal(ks[4], (DEPTH, IDX_DIM), f32)
    idx_k_norm_b = 0.02 * jax.random.normal(ks[5], (DEPTH, IDX_DIM), f32)
    w_uk = jax.random.normal(ks[6], (DEPTH, KV_RANK, A_HEADS, A_HEAD_DIM), f32) * (KV_RANK ** -0.5)
    w_uv = jax.random.normal(ks[7], (DEPTH, KV_RANK, A_HEADS, A_HEAD_DIM), f32) * (KV_RANK ** -0.5)
    v_norm_g = 1.0 + 0.02 * jax.random.normal(ks[8], (DEPTH, B_GROUP_DIM * B_GROUPS), f32)
    v_norm_b = 0.02 * jax.random.normal(ks[9], (DEPTH, B_GROUP_DIM * B_GROUPS), f32)
    w_s = jax.random.normal(ks[10], (DEPTH, B_GROUPS, CHUNK, CHUNK), f32) * (CHUNK ** -0.5)
    b_s = 1.0 + 0.01 * jax.random.normal(ks[11], (DEPTH, B_GROUPS, CHUNK), f32)
    w_out = jax.random.normal(ks[12], (DEPTH, MIX_WIDTH, D_MODEL), f32) * (MIX_WIDTH ** -0.5)
    final_norm_g = 1.0 + 0.02 * jax.random.normal(ks[13], (D_MODEL,), f32)
    return {"x": x, "norm_g": norm_g, "w_in": w_in, "kv_norm_g": kv_norm_g,
            "idx_k_norm_g": idx_k_norm_g, "idx_k_norm_b": idx_k_norm_b, "w_uk": w_uk, "w_uv": w_uv,
            "v_norm_g": v_norm_g, "v_norm_b": v_norm_b, "w_s": w_s, "b_s": b_s,
            "w_out": w_out, "final_norm_g": final_norm_g}


def reference(x, norm_g, w_in, kv_norm_g, idx_k_norm_g, idx_k_norm_b, w_uk, w_uv,
              v_norm_g, v_norm_b, w_s, b_s, w_out, final_norm_g):
    bsz, L = x.shape[0], x.shape[1]
    offs = _offsets()
    for l in range(DEPTH):
        h = rmsnorm(x, norm_g[l])
        proj = h @ w_in[l]
        q, c_kv, gate_a, q_idx, k_idx, w_idx, u, v, gate_b = jnp.split(proj, offs, axis=-1)
        q = q.reshape(bsz, L, A_HEADS, A_HEAD_DIM)
        c_kv = rmsnorm(c_kv, kv_norm_g[l])
        q_idx = q_idx.reshape(bsz, L, IDX_HEADS, IDX_DIM)
        k_idx = layernorm(k_idx, idx_k_norm_g[l], idx_k_norm_b[l])
        w_idx = w_idx * (IDX_HEADS ** -0.5)
        o_a = dsa_attention(q, c_kv, q_idx, k_idx, w_idx, w_uk[l], w_uv[l]).reshape(bsz, L, A_WIDTH)
        y_a = o_a * jax.nn.silu(gate_a)
        u = jax.nn.gelu(u, approximate=False).reshape(bsz, L, B_GROUPS, B_GROUP_DIM)
        v = layernorm(jax.nn.gelu(v, approximate=False), v_norm_g[l], v_norm_b[l])
        v = v.reshape(bsz, L, B_GROUPS, B_GROUP_DIM)
        o_b = spatial_gating(u, v, w_s[l], b_s[l]).reshape(bsz, L, B_WIDTH)
        y_b = o_b * jax.nn.silu(gate_b)
        x = x + jnp.concatenate([y_a, y_b], axis=-1) @ w_out[l]
    return rmsnorm(x, final_norm_g)
```

```python
import functools

import jax
import jax.numpy as jnp
from jax import lax
from jax.experimental import pallas as pl
from jax.experimental.pallas import tpu as pltpu

F32, BF16, I32 = jnp.float32, jnp.bfloat16, jnp.int32

EPS = 1e-6
TOPK_MAX = 256
SQRT_HALF = 0.7071067811865476
LOG2E = 1.4426950408889634

V7X_LANES = 128
V7X_SUBLANES = 8
V7X_VMEM_LIMIT_BYTES = 56 * 2**20

INT_MIN = -(2**31)
MASK_NEG = -1e30


def _cparams(*sem):
    return pltpu.CompilerParams(dimension_semantics=sem, vmem_limit_bytes=V7X_VMEM_LIMIT_BYTES)


def _tile(n, want):
    t = min(n, want)
    while n % t:
        t //= 2
    return t


def _rmsnorm_kernel(x_ref, g_ref, o_ref):
    x = x_ref[...]
    ms = jnp.mean(x * x, axis=-1, keepdims=True)
    o_ref[...] = (x * lax.rsqrt(ms + EPS) * g_ref[...]).astype(o_ref.dtype)


def _rmsnorm(x, g, out_dtype):
    m, d = x.shape
    tm = _tile(m, 256)
    return pl.pallas_call(
        _rmsnorm_kernel,
        grid=(m // tm,),
        in_specs=[pl.BlockSpec((tm, d), lambda i: (i, 0)), pl.BlockSpec((1, d), lambda i: (0, 0))],
        out_specs=pl.BlockSpec((tm, d), lambda i: (i, 0)),
        out_shape=jax.ShapeDtypeStruct((m, d), out_dtype),
        compiler_params=_cparams("parallel"),
        name="rmsnorm",
    )(x, g.reshape(1, d))


def _gelu(x):
    return 0.5 * x * (1.0 + lax.erf(x * SQRT_HALF))


_ACTS = {"none": lambda x: x, "silu": jax.nn.silu, "gelu": _gelu}


def _proj_kernel(h_ref, w_ref, o_ref, *, act):
    acc = jnp.dot(h_ref[...], w_ref[...], preferred_element_type=F32)
    o_ref[...] = _ACTS[act](acc).astype(o_ref.dtype)


def _proj(h, w, act, out_dtype, name):
    m, k = h.shape
    n = w.shape[1]
    tm, tn = _tile(m, 1024), _tile(n, 512)
    return pl.pallas_call(
        functools.partial(_proj_kernel, act=act),
        grid=(n // tn, m // tm),
        in_specs=[pl.BlockSpec((tm, k), lambda j, i: (i, 0)), pl.BlockSpec((k, tn), lambda j, i: (0, j))],
        out_specs=pl.BlockSpec((tm, tn), lambda j, i: (i, j)),
        out_shape=jax.ShapeDtypeStruct((m, n), out_dtype),
        compiler_params=_cparams("parallel", "parallel"),
        name=name,
    )(h, w)


def _kv_kernel(h_ref, w_ref, g_ref, c_ref, ct_ref, *, kb):
    c = jnp.dot(h_ref[...], w_ref[...], preferred_element_type=F32)
    ms = jnp.mean(c * c, axis=-1, keepdims=True)
    cn = c * lax.rsqrt(ms + EPS) * g_ref[...]
    c_ref[...] = cn.astype(c_ref.dtype)
    cnt = cn.T
    for j in range(ct_ref.shape[0]):
        ct_ref[j] = cnt[:, j * kb:(j + 1) * kb].astype(ct_ref.dtype)


def _kv_latent(h, w, g, kb):
    m, k = h.shape
    c = w.shape[1]
    tm = _tile(m, 512)
    assert tm % kb == 0
    return pl.pallas_call(
        functools.partial(_kv_kernel, kb=kb),
        grid=(m // tm,),
        in_specs=[pl.BlockSpec((tm, k), lambda i: (i, 0)), pl.BlockSpec((k, c), lambda i: (0, 0)),
                  pl.BlockSpec((1, c), lambda i: (0, 0))],
        out_specs=[pl.BlockSpec((tm, c), lambda i: (i, 0)), pl.BlockSpec((tm // kb, c, kb), lambda i: (i, 0, 0))],
        out_shape=[jax.ShapeDtypeStruct((m, c), BF16), jax.ShapeDtypeStruct((m // kb, c, kb), BF16)],
        compiler_params=_cparams("parallel"),
        name="kv_latent",
    )(h, w, g.reshape(1, c))


def _idx_kernel(h_ref, wk_ref, wwt_ref, g_ref, b_ref, k_ref, wt_ref, *, wscale):
    h = h_ref[...]
    k = jnp.dot(h, wk_ref[...], preferred_element_type=F32)
    mu = jnp.mean(k, axis=-1, keepdims=True)
    kc = k - mu
    var = jnp.mean(kc * kc, axis=-1, keepdims=True)
    k_ref[...] = (kc * lax.rsqrt(var + EPS) * g_ref[...] + b_ref[...]).astype(k_ref.dtype)
    wt = lax.dot_general(wwt_ref[...], h, (((1,), (1,)), ((), ())), preferred_element_type=F32)
    wt_ref[...] = wt * wscale


def _idx_small(h, wk, wwt, g, b, wscale):
    m, k = h.shape
    di, hi = wk.shape[1], wwt.shape[0]
    tm = _tile(m, 512)
    return pl.pallas_call(
        functools.partial(_idx_kernel, wscale=wscale),
        grid=(m // tm,),
        in_specs=[pl.BlockSpec((tm, k), lambda i: (i, 0)), pl.BlockSpec((k, di), lambda i: (0, 0)),
                  pl.BlockSpec((hi, k), lambda i: (0, 0)), pl.BlockSpec((1, di), lambda i: (0, 0)),
                  pl.BlockSpec((1, di), lambda i: (0, 0))],
        out_specs=[pl.BlockSpec((tm, di), lambda i: (i, 0)), pl.BlockSpec((hi, tm), lambda i: (0, i))],
        out_shape=[jax.ShapeDtypeStruct((m, di), BF16), jax.ShapeDtypeStruct((hi, m), F32)],
        compiler_params=_cparams("parallel"),
        name="idx_small",
    )(h, wk, wwt, g.reshape(1, di), b.reshape(1, di))


def _spatial_kernel(u_ref, v_ref, gb_ref, ws_ref, bst_ref, vg_ref, vb_ref, o_ref, *, chunk, groups):
    tm, bw = v_ref.shape
    e = bw // groups
    v = v_ref[...]
    mu = jnp.mean(v, axis=-1, keepdims=True)
    vc = v - mu
    var = jnp.mean(vc * vc, axis=-1, keepdims=True)
    vn = (vc * lax.rsqrt(var + EPS) * vg_ref[...] + vb_ref[...]).astype(BF16)
    row = lax.broadcasted_iota(I32, (chunk, chunk), 0)
    col = lax.broadcasted_iota(I32, (chunk, chunk), 1)
    causal = col <= row
    for g in range(groups):
        wsg = jnp.where(causal, ws_ref[g], 0.0).astype(BF16)
        bias = jnp.broadcast_to(bst_ref[:, g:g + 1], (chunk, e))
        for c in range(tm // chunk):
            rows, cols = slice(c * chunk, (c + 1) * chunk), slice(g * e, (g + 1) * e)
            mixed = jnp.dot(wsg, vn[rows, cols], preferred_element_type=F32) + bias
            o_ref[rows, cols] = (u_ref[rows, cols] * mixed * gb_ref[rows, cols]).astype(o_ref.dtype)


def _spatial(u, v, gate, w_s, b_s, vg, vb):
    m, bw = u.shape
    groups, chunk, _ = w_s.shape
    tm = 2 * chunk if m % (2 * chunk) == 0 else chunk
    row = lambda i: (i, 0)
    return pl.pallas_call(
        functools.partial(_spatial_kernel, chunk=chunk, groups=groups),
        grid=(m // tm,),
        in_specs=[pl.BlockSpec((tm, bw), row), pl.BlockSpec((tm, bw), row), pl.BlockSpec((tm, bw), row),
                  pl.BlockSpec((groups, chunk, chunk), lambda i: (0, 0, 0)),
                  pl.BlockSpec((chunk, groups), lambda i: (0, 0)),
                  pl.BlockSpec((1, bw), lambda i: (0, 0)), pl.BlockSpec((1, bw), lambda i: (0, 0))],
        out_specs=pl.BlockSpec((tm, bw), row),
        out_shape=jax.ShapeDtypeStruct((m, bw), BF16),
        compiler_params=_cparams("parallel"),
        name="spatial_gating",
    )(u, v, gate, w_s, b_s.T, vg.reshape(1, bw), vb.reshape(1, bw))


IDX_HEAD_GROUP = 4
ATT_HEAD_GROUP = 4
COUNT_ROWS = 256


def _attn_kernel(q_ref, qidx_ref, wt_ref, gate_ref, c_ref, ct_ref, kidx_ref, wuk_ref, wuvt_ref, o_ref,
                 key_ref, qs_ref, qlat_ref, acc_ref, m_ref, l_ref, *, qb, kb, topk, sm_scale):
    heads, cdim, dh = wuk_ref.shape
    iheads, di = wt_ref.shape[0], kidx_ref.shape[1]
    i = pl.program_id(1)
    n_kb = ((i + 1) * qb + kb - 1) // kb
    nt = (((1,), (1,)), ((), ()))

    for h in range(iheads):
        qs_ref[h * qb:(h + 1) * qb, :] = qidx_ref[:, h * di:(h + 1) * di]
    for h in range(heads):
        ql = lax.dot_general(wuk_ref[h], q_ref[:, h * dh:(h + 1) * dh], nt, preferred_element_type=F32)
        qlat_ref[:, h * qb:(h + 1) * qb] = ql.astype(qlat_ref.dtype)

    wt = wt_ref[...]
    qpos = i * qb + lax.broadcasted_iota(I32, (1, qb), 1)

    def score_block(kbi, masked):
        r0 = pl.multiple_of(kbi * kb, kb)
        kblk = kidx_ref[pl.ds(r0, kb), :]
        score = jnp.zeros((kb, qb), F32)
        for hg in range(iheads // IDX_HEAD_GROUP):
            h0 = hg * IDX_HEAD_GROUP
            lt = lax.dot_general(kblk, qs_ref[h0 * qb:(h0 + IDX_HEAD_GROUP) * qb, :], nt,
                                 preferred_element_type=F32)
            for hh in range(IDX_HEAD_GROUP):
                score = score + jnp.maximum(lt[:, hh * qb:(hh + 1) * qb], 0.0) * wt[h0 + hh:h0 + hh + 1, :]
        bits = pltpu.bitcast(score, I32)
        key = bits ^ ((bits >> 31) & 0x7FFFFFFF)
        key = jnp.where(key == -1, 0, key)
        if masked:
            kpos = r0 + lax.broadcasted_iota(I32, (kb, 1), 0)
            key = jnp.where(kpos <= qpos, key, INT_MIN)
        key_ref[pl.ds(r0, kb), :] = key

    def score_body(kbi, carry):
        score_block(kbi, False)
        return carry

    lax.fori_loop(0, n_kb - 1, score_body, 0)
    score_block(n_kb - 1, True)

    n_chunks = n_kb * (kb // COUNT_ROWS)
    k_eff = jnp.minimum(qpos + 1, topk)

    def count(pred):
        def body(r, acc):
            r0 = pl.multiple_of(r * COUNT_ROWS, COUNT_ROWS)
            hit = pred(key_ref[pl.ds(r0, COUNT_ROWS), :], r0).astype(I32)
            return acc + jnp.sum(hit.reshape(COUNT_ROWS // V7X_SUBLANES, V7X_SUBLANES, qb), axis=0)
        acc = lax.fori_loop(0, n_chunks, body, jnp.zeros((V7X_SUBLANES, qb), I32))
        return jnp.sum(acc, axis=0, keepdims=True)

    def thr_body(b, thr):
        cand = thr + lax.shift_left(jnp.int32(1), 31 - b)
        cnt = count(lambda blk, r0: blk >= cand)
        return jnp.where(cnt >= k_eff, cand, thr)

    thr = lax.fori_loop(0, 32, thr_body, jnp.full((1, qb), INT_MIN, I32))

    n_gt = count(lambda blk, r0: blk > thr)
    n_ge = count(lambda blk, r0: blk >= thr)
    need = k_eff - n_gt

    @pl.when(jnp.max(n_ge - k_eff) > 0)
    def _():
        def rows(r0):
            return r0 + lax.broadcasted_iota(I32, (COUNT_ROWS, 1), 0)

        def idx_body(b, lim):
            cand = lim + lax.shift_left(jnp.int32(1), 30 - b)
            cnt = count(lambda blk, r0: (blk == thr) & (rows(r0) < cand))
            return jnp.where(cnt < need, cand, lim)

        lim = lax.fori_loop(0, 31, idx_body, jnp.zeros((1, qb), I32))

        def drop_body(r, carry):
            r0 = pl.multiple_of(r * COUNT_ROWS, COUNT_ROWS)
            blk = key_ref[pl.ds(r0, COUNT_ROWS), :]
            key_ref[pl.ds(r0, COUNT_ROWS), :] = jnp.where((blk == thr) & (rows(r0) > lim), INT_MIN, blk)
            return carry

        lax.fori_loop(0, n_chunks, drop_body, 0)

    m_ref[...] = jnp.full(m_ref.shape, MASK_NEG, F32)
    l_ref[...] = jnp.zeros(l_ref.shape, F32)
    acc_ref[...] = jnp.zeros(acc_ref.shape, F32)
    cs = sm_scale * LOG2E
    gw = ATT_HEAD_GROUP * qb

    def att_body(kbi, carry):
        r0 = pl.multiple_of(kbi * kb, kb)
        cblk = c_ref[pl.ds(r0, kb), :]
        ctblk = ct_ref[kbi]
        bias = jnp.where(key_ref[pl.ds(r0, kb), :] >= thr, 0.0, MASK_NEG)
        bias = jnp.concatenate([bias] * ATT_HEAD_GROUP, axis=1)
        for hg in range(heads // ATT_HEAD_GROUP):
            cols = slice(hg * gw, (hg + 1) * gw)
            s = jnp.dot(cblk, qlat_ref[:, cols], preferred_element_type=F32) + bias
            m_old = m_ref[:, cols]
            m_new = jnp.maximum(m_old, jnp.max(s, axis=0, keepdims=True))
            alpha = jnp.exp2((m_old - m_new) * cs)
            p = jnp.exp2((s - m_new) * cs)
            l_ref[:, cols] = l_ref[:, cols] * alpha + jnp.sum(p, axis=0, keepdims=True)
            pv = jnp.dot(ctblk, p.astype(ctblk.dtype), preferred_element_type=F32)
            acc_ref[:, cols] = acc_ref[:, cols] * alpha + pv
            m_ref[:, cols] = m_new
        return carry

    lax.fori_loop(0, n_kb, att_body, 0)

    inv_l = 1.0 / l_ref[...]
    for h in range(heads):
        cols = slice(h * qb, (h + 1) * qb)
        o_lat = (acc_ref[:, cols] * inv_l[:, cols]).astype(wuvt_ref.dtype)
        out_t = jnp.dot(wuvt_ref[h], o_lat, preferred_element_type=F32)
        o_ref[:, h * dh:(h + 1) * dh] = (out_t.T * gate_ref[:, h * dh:(h + 1) * dh]).astype(o_ref.dtype)


def _sparse_attention(q, qidx, wt, gate, c, ct, kidx, wuk, wuvt, *, batch, kb):
    m, aw = q.shape
    seq = m // batch
    heads, cdim, dh = wuk.shape
    iheads, di = wt.shape[0], kidx.shape[1]
    qb = V7X_LANES
    nq = seq // qb
    assert seq % kb == 0 and kb % COUNT_ROWS == 0 and iheads % IDX_HEAD_GROUP == 0 and heads % ATT_HEAD_GROUP == 0
    topk = min(TOPK_MAX, seq // 4)
    qrow = lambda b, i: (b * nq + i, 0)
    resident = dict(pipeline_mode=pl.Buffered(1))
    return pl.pallas_call(
        functools.partial(_attn_kernel, qb=qb, kb=kb, topk=topk, sm_scale=dh ** -0.5),
        grid=(batch, nq),
        in_specs=[pl.BlockSpec((qb, aw), qrow),
                  pl.BlockSpec((qb, iheads * di), qrow),
                  pl.BlockSpec((iheads, qb), lambda b, i: (0, b * nq + i)),
                  pl.BlockSpec((qb, aw), qrow),
                  pl.BlockSpec((seq, cdim), lambda b, i: (b, 0), **resident),
                  pl.BlockSpec((seq // kb, cdim, kb), lambda b, i: (b, 0, 0), **resident),
                  pl.BlockSpec((seq, di), lambda b, i: (b, 0), **resident),
                  pl.BlockSpec((heads, cdim, dh), lambda b, i: (0, 0, 0), **resident),
                  pl.BlockSpec((heads, dh, cdim), lambda b, i: (0, 0, 0), **resident)],
        out_specs=pl.BlockSpec((qb, aw), qrow),
        out_shape=jax.ShapeDtypeStruct((m, aw), BF16),
        scratch_shapes=[pltpu.VMEM((seq, qb), I32),
                        pltpu.VMEM((iheads * qb, di), BF16),
                        pltpu.VMEM((cdim, heads * qb), BF16),
                        pltpu.VMEM((cdim, heads * qb), F32),
                        pltpu.VMEM((1, heads * qb), F32),
                        pltpu.VMEM((1, heads * qb), F32)],
        compiler_params=_cparams("parallel", "arbitrary"),
        name="sparse_attention",
    )(q, qidx, wt, gate, c, ct, kidx, wuk, wuvt)


def _outproj_kernel(ya_ref, yb_ref, wa_ref, wb_ref, x_ref, o_ref):
    acc = jnp.dot(ya_ref[...], wa_ref[...], preferred_element_type=F32)
    acc = acc + jnp.dot(yb_ref[...], wb_ref[...], preferred_element_type=F32)
    o_ref[...] = x_ref[...] + acc


def _outproj(ya, yb, wa, wb, x):
    m, ka = ya.shape
    kbw = yb.shape[1]
    n = wa.shape[1]
    tm, tn = _tile(m, 1024), _tile(n, 512)
    return pl.pallas_call(
        _outproj_kernel,
        grid=(n // tn, m // tm),
        in_specs=[pl.BlockSpec((tm, ka), lambda j, i: (i, 0)), pl.BlockSpec((tm, kbw), lambda j, i: (i, 0)),
                  pl.BlockSpec((ka, tn), lambda j, i: (0, j)), pl.BlockSpec((kbw, tn), lambda j, i: (0, j)),
                  pl.BlockSpec((tm, tn), lambda j, i: (i, j))],
        out_specs=pl.BlockSpec((tm, tn), lambda j, i: (i, j)),
        out_shape=jax.ShapeDtypeStruct((m, n), F32),
        compiler_params=_cparams("parallel", "parallel"),
        name="outproj",
    )(ya, yb, wa, wb, x)


def kernel(x, norm_g, w_in, kv_norm_g, idx_k_norm_g, idx_k_norm_b, w_uk, w_uv, v_norm_g, v_norm_b, w_s, b_s,
           w_out, final_norm_g):
    batch, seq, d = x.shape
    depth = w_in.shape[0]
    cdim, heads, dh = w_uk.shape[1:]
    aw = heads * dh
    di = idx_k_norm_g.shape[1]
    bw = v_norm_g.shape[1]
    iheads = (w_in.shape[2] - 2 * aw - cdim - di - 3 * bw) // (di + 1)
    kb = _tile(seq, 512)

    names = ("q", "c_kv", "gate_a", "q_idx", "k_idx", "w_idx", "u", "v", "gate_b")
    sizes = (aw, cdim, aw, iheads * di, di, iheads, bw, bw, bw)
    seg, off = {}, 0
    for nm, sz in zip(names, sizes):
        seg[nm] = (off, off + sz)
        off += sz

    xf = x.reshape(batch * seq, d)
    for l in range(depth):
        w = {nm: w_in[l][:, a:b].astype(BF16) for nm, (a, b) in seg.items()}
        h = _rmsnorm(xf, norm_g[l], BF16)
        q = _proj(h, w["q"], "none", BF16, "proj_q")
        c, ct = _kv_latent(h, w["c_kv"], kv_norm_g[l], kb)
        gate_a = _proj(h, w["gate_a"], "silu", F32, "proj_gate_a")
        qidx = _proj(h, w["q_idx"], "none", BF16, "proj_q_idx")
        kidx, wt = _idx_small(h, w["k_idx"], w["w_idx"].T, idx_k_norm_g[l], idx_k_norm_b[l],
                              (iheads ** -0.5) * (di ** -0.5))
        wuk = jnp.transpose(w_uk[l], (1, 0, 2)).astype(BF16)
        wuvt = jnp.transpose(w_uv[l], (1, 2, 0)).astype(BF16)
        y_a = _sparse_attention(q, qidx, wt, gate_a, c, ct, kidx, wuk, wuvt, batch=batch, kb=kb)
        u = _proj(h, w["u"], "gelu", F32, "proj_u")
        v = _proj(h, w["v"], "gelu", F32, "proj_v")
        gate_b = _proj(h, w["gate_b"], "silu", F32, "proj_gate_b")
        y_b = _spatial(u, v, gate_b, w_s[l], b_s[l], v_norm_g[l], v_norm_b[l])
        wo = w_out[l].astype(BF16)
        xf = _outproj(y_a, y_b, wo[:aw], wo[aw:], xf)
    return _rmsnorm(xf, final_norm_g, x.dtype).reshape(batch, seq, d)
```

```python
import functools

import jax
import jax.numpy as jnp
from jax import lax
from jax.experimental import pallas as pl
from jax.experimental.pallas import tpu as pltpu

F32, BF16, I32 = jnp.float32, jnp.bfloat16, jnp.int32

EPS = 1e-6
TOPK_MAX = 256
SQRT_HALF = 0.7071067811865476
LOG2E = 1.4426950408889634

V7X_LANES = 128
V7X_SUBLANES = 8
V7X_VMEM_LIMIT_BYTES = 56 * 2**20

INT_MIN = -(2**31)
MASK_NEG = -1e30


def _cparams(*sem):
    return pltpu.CompilerParams(dimension_semantics=sem, vmem_limit_bytes=V7X_VMEM_LIMIT_BYTES)


def _tile(n, want):
    t = min(n, want)
    while n % t:
        t //= 2
    return t


def _rmsnorm_kernel(x_ref, g_ref, o_ref):
    x = x_ref[...]
    ms = jnp.mean(x * x, axis=-1, keepdims=True)
    o_ref[...] = (x * lax.rsqrt(ms + EPS) * g_ref[...]).astype(o_ref.dtype)


def _rmsnorm(x, g, out_dtype):
    m, d = x.shape
    tm = _tile(m, 256)
    return pl.pallas_call(
        _rmsnorm_kernel,
        grid=(m // tm,),
        in_specs=[pl.BlockSpec((tm, d), lambda i: (i, 0)), pl.BlockSpec((1, d), lambda i: (0, 0))],
        out_specs=pl.BlockSpec((tm, d), lambda i: (i, 0)),
        out_shape=jax.ShapeDtypeStruct((m, d), out_dtype),
        compiler_params=_cparams("parallel"),
        name="rmsnorm",
    )(x, g.reshape(1, d))


def _gelu(x):
    return 0.5 * x * (1.0 + lax.erf(x * SQRT_HALF))


_ACTS = {"none": lambda x: x, "silu": jax.nn.silu, "gelu": _gelu}


def _proj_kernel(h_ref, w_ref, o_ref, *, act):
    acc = jnp.dot(h_ref[...], w_ref[...], preferred_element_type=F32)
    o_ref[...] = _ACTS[act](acc).astype(o_ref.dtype)


def _proj(h, w, act, out_dtype, name):
    m, k = h.shape
    n = w.shape[1]
    tm, tn = _tile(m, 1024), _tile(n, 512)
    return pl.pallas_call(
        functools.partial(_proj_kernel, act=act),
        grid=(n // tn, m // tm),
        in_specs=[pl.BlockSpec((tm, k), lambda j, i: (i, 0)), pl.BlockSpec((k, tn), lambda j, i: (0, j))],
        out_specs=pl.BlockSpec((tm, tn), lambda j, i: (i, j)),
        out_shape=jax.ShapeDtypeStruct((m, n), out_dtype),
        compiler_params=_cparams("parallel", "parallel"),
        name=name,
    )(h, w)


def _kv_kernel(h_ref, w_ref, g_ref, c_ref, ct_ref, *, kb):
    c = jnp.dot(h_ref[...], w_ref[...], preferred_element_type=F32)
    ms = jnp.mean(c * c, axis=-1, keepdims=True)
    cn = c * lax.rsqrt(ms + EPS) * g_ref[...]
    c_ref[...] = cn.astype(c_ref.dtype)
    cnt = cn.T
    for j in range(ct_ref.shape[0]):
        ct_ref[j] = cnt[:, j * kb:(j + 1) * kb].astype(ct_ref.dtype)


def _kv_latent(h, w, g, kb):
    m, k = h.shape
    c = w.shape[1]
    tm = _tile(m, 512)
    assert tm % kb == 0
    return pl.pallas_call(
        functools.partial(_kv_kernel, kb=kb),
        grid=(m // tm,),
        in_specs=[pl.BlockSpec((tm, k), lambda i: (i, 0)), pl.BlockSpec((k, c), lambda i: (0, 0)),
                  pl.BlockSpec((1, c), lambda i: (0, 0))],
        out_specs=[pl.BlockSpec((tm, c), lambda i: (i, 0)), pl.BlockSpec((tm // kb, c, kb), lambda i: (i, 0, 0))],
        out_shape=[jax.ShapeDtypeStruct((m, c), BF16), jax.ShapeDtypeStruct((m // kb, c, kb), BF16)],
        compiler_params=_cparams("parallel"),
        name="kv_latent",
    )(h, w, g.reshape(1, c))


def _idx_kernel(h_ref, wk_ref, wwt_ref, g_ref, b_ref, k_ref, wt_ref, *, wscale):
    h = h_ref[...]
    k = jnp.dot(h, wk_ref[...], preferred_element_type=F32)
    mu = jnp.mean(k, axis=-1, keepdims=True)
    kc = k - mu
    var = jnp.mean(kc * kc, axis=-1, keepdims=True)
    k_ref[...] = (kc * lax.rsqrt(var + EPS) * g_ref[...] + b_ref[...]).astype(k_ref.dtype)
    wt = lax.dot_general(wwt_ref[...], h, (((1,), (1,)), ((), ())), preferred_element_type=F32)
    wt_ref[...] = wt * wscale


def _idx_small(h, wk, wwt, g, b, wscale):
    m, k = h.shape
    di, hi = wk.shape[1], wwt.shape[0]
    tm = _tile(m, 512)
    return pl.pallas_call(
        functools.partial(_idx_kernel, wscale=wscale),
        grid=(m // tm,),
        in_specs=[pl.BlockSpec((tm, k), lambda i: (i, 0)), pl.BlockSpec((k, di), lambda i: (0, 0)),
                  pl.BlockSpec((hi, k), lambda i: (0, 0)), pl.BlockSpec((1, di), lambda i: (0, 0)),
                  pl.BlockSpec((1, di), lambda i: (0, 0))],
        out_specs=[pl.BlockSpec((tm, di), lambda i: (i, 0)), pl.BlockSpec((hi, tm), lambda i: (0, i))],
        out_shape=[jax.ShapeDtypeStruct((m, di), BF16), jax.ShapeDtypeStruct((hi, m), F32)],
        compiler_params=_cparams("parallel"),
        name="idx_small",
    )(h, wk, wwt, g.reshape(1, di), b.reshape(1, di))


def _spatial_kernel(u_ref, v_ref, gb_ref, ws_ref, bst_ref, vg_ref, vb_ref, o_ref, *, chunk, groups):
    tm, bw = v_ref.shape
    e = bw // groups
    v = v_ref[...]
    mu = jnp.mean(v, axis=-1, keepdims=True)
    vc = v - mu
    var = jnp.mean(vc * vc, axis=-1, keepdims=True)
    vn = (vc * lax.rsqrt(var + EPS) * vg_ref[...] + vb_ref[...]).astype(BF16)
    row = lax.broadcasted_iota(I32, (chunk, chunk), 0)
    col = lax.broadcasted_iota(I32, (chunk, chunk), 1)
    causal = col <= row
    for g in range(groups):
        wsg = jnp.where(causal, ws_ref[g], 0.0).astype(BF16)
        bias = jnp.broadcast_to(bst_ref[:, g:g + 1], (chunk, e))
        for c in range(tm // chunk):
            rows, cols = slice(c * chunk, (c + 1) * chunk), slice(g * e, (g + 1) * e)
            mixed = jnp.dot(wsg, vn[rows, cols], preferred_element_type=F32) + bias
            o_ref[rows, cols] = (u_ref[rows, cols] * mixed * gb_ref[rows, cols]).astype(o_ref.dtype)


def _spatial(u, v, gate, w_s, b_s, vg, vb):
    m, bw = u.shape
    groups, chunk, _ = w_s.shape
    tm = 2 * chunk if m % (2 * chunk) == 0 else chunk
    row = lambda i: (i, 0)
    return pl.pallas_call(
        functools.partial(_spatial_kernel, chunk=chunk, groups=groups),
        grid=(m // tm,),
        in_specs=[pl.BlockSpec((tm, bw), row), pl.BlockSpec((tm, bw), row), pl.BlockSpec((tm, bw), row),
                  pl.BlockSpec((groups, chunk, chunk), lambda i: (0, 0, 0)),
                  pl.BlockSpec((chunk, groups), lambda i: (0, 0)),
                  pl.BlockSpec((1, bw), lambda i: (0, 0)), pl.BlockSpec((1, bw), lambda i: (0, 0))],
        out_specs=pl.BlockSpec((tm, bw), row),
        out_shape=jax.ShapeDtypeStruct((m, bw), BF16),
        compiler_params=_cparams("parallel"),
        name="spatial_gating",
    )(u, v, gate, w_s, b_s.T, vg.reshape(1, bw), vb.reshape(1, bw))


IDX_HEAD_GROUP = 4
ATT_HEAD_GROUP = 4
WORD_BITS = 32
CHUNK_ROWS = WORD_BITS * V7X_SUBLANES


def _bit_transpose32(a):
    a = list(a)
    m, j = 0x0000FFFF, 16
    while j:
        k = 0
        while k < WORD_BITS:
            t = (a[k] ^ lax.shift_right_logical(a[k + j], jnp.int32(j))) & jnp.int32(m)
            a[k] = a[k] ^ t
            a[k + j] = a[k + j] ^ lax.shift_left(t, jnp.int32(j))
            k = (k + j + 1) & ~j
        j >>= 1
        m = (m ^ (m << j)) & 0xFFFFFFFF
    return a


def _attn_kernel(q_ref, qidx_ref, wt_ref, gate_ref, c_ref, ct_ref, kidx_ref, wuk_ref, wuvt_ref, o_ref,
                 planes_ref, alive_ref, sel_ref, qs_ref, qlat_ref, acc_ref, m_ref, l_ref,
                 *, qb, kb, topk, sm_scale, seq_bits):
    heads, cdim, dh = wuk_ref.shape
    iheads, di = wt_ref.shape[0], kidx_ref.shape[1]
    cpk = kb // CHUNK_ROWS
    i = pl.program_id(1)
    n_kb = ((i + 1) * qb + kb - 1) // kb
    nt = (((1,), (1,)), ((), ()))

    for h in range(iheads):
        qs_ref[h * qb:(h + 1) * qb, :] = qidx_ref[:, h * di:(h + 1) * di]
    for h in range(heads):
        ql = lax.dot_general(wuk_ref[h], q_ref[:, h * dh:(h + 1) * dh], nt, preferred_element_type=F32)
        qlat_ref[:, h * qb:(h + 1) * qb] = ql.astype(qlat_ref.dtype)

    wt = wt_ref[...]
    qpos = i * qb + lax.broadcasted_iota(I32, (1, qb), 1)

    @pl.when(i == 0)
    def _():
        planes_ref[...] = jnp.zeros(planes_ref.shape, I32)

    def score_block(kbi, masked):
        r0 = pl.multiple_of(kbi * kb, kb)
        kblk = kidx_ref[pl.ds(r0, kb), :]
        score = jnp.zeros((kb, qb), F32)
        for hg in range(iheads // IDX_HEAD_GROUP):
            h0 = hg * IDX_HEAD_GROUP
            lt = lax.dot_general(kblk, qs_ref[h0 * qb:(h0 + IDX_HEAD_GROUP) * qb, :], nt,
                                 preferred_element_type=F32)
            for hh in range(IDX_HEAD_GROUP):
                score = score + jnp.maximum(lt[:, hh * qb:(hh + 1) * qb], 0.0) * wt[h0 + hh:h0 + hh + 1, :]
        bits = pltpu.bitcast(score, I32)
        key = bits ^ ((bits >> 31) | INT_MIN)
        key = jnp.where(key == 0x7FFFFFFF, INT_MIN, key)
        if masked:
            kpos = r0 + lax.broadcasted_iota(I32, (kb, 1), 0)
            key = jnp.where(kpos <= qpos, key, 0)
        for cc in range(cpk):
            rows = [key[cc * CHUNK_ROWS + j * V7X_SUBLANES:cc * CHUNK_ROWS + (j + 1) * V7X_SUBLANES, :]
                    for j in range(WORD_BITS)]
            planes = _bit_transpose32(rows)
            for b in range(WORD_BITS):
                planes_ref[b, kbi * cpk + cc] = planes[b]

    def score_body(kbi, carry):
        score_block(kbi, False)
        return carry

    lax.fori_loop(0, n_kb - 1, score_body, 0)
    score_block(n_kb - 1, True)

    def popsum(words):
        return jnp.sum(jnp.sum(lax.population_count(words), axis=0), axis=0, keepdims=True)

    cidx = lax.broadcasted_iota(I32, alive_ref.shape, 0)
    alive_ref[...] = jnp.where(cidx < n_kb * cpk, -1, 0)
    sel_ref[...] = jnp.zeros(sel_ref.shape, I32)

    def radix_body(b, k_rem):
        plane = planes_ref[b]
        alive = alive_ref[...]
        ones = alive & plane
        cnt = popsum(ones)
        take1 = cnt >= k_rem
        alive_ref[...] = jnp.where(take1, ones, alive & ~plane)
        sel_ref[...] = jnp.where(take1, sel_ref[...], sel_ref[...] | ones)
        return jnp.where(take1, k_rem, k_rem - cnt)

    need = lax.fori_loop(0, WORD_BITS, radix_body, jnp.minimum(qpos + 1, topk))

    @pl.when(jnp.max(popsum(alive_ref[...]) - need) > 0)
    def _():
        ties = alive_ref[...]
        row0 = cidx * CHUNK_ROWS + lax.broadcasted_iota(I32, alive_ref.shape, 1)

        def rows_below(lim):
            nj = jnp.clip(lax.shift_right_arithmetic(lim - row0 + (V7X_SUBLANES - 1), 3), 0, WORD_BITS)
            return jnp.where(nj <= 0, 0, lax.shift_left(jnp.int32(-1), WORD_BITS - jnp.maximum(nj, 1)))

        def idx_body(b, lim):
            cand = lim + lax.shift_left(jnp.int32(1), seq_bits - 1 - b)
            return jnp.where(popsum(ties & rows_below(cand)) < need, cand, lim)

        lim = lax.fori_loop(0, seq_bits, idx_body, jnp.zeros((1, qb), I32))
        alive_ref[...] = ties & rows_below(lim + 1)

    sel_ref[...] = sel_ref[...] | alive_ref[...]

    m_ref[...] = jnp.full(m_ref.shape, MASK_NEG, F32)
    l_ref[...] = jnp.zeros(l_ref.shape, F32)
    acc_ref[...] = jnp.zeros(acc_ref.shape, F32)
    cs = sm_scale * LOG2E
    gw = ATT_HEAD_GROUP * qb

    def att_body(kbi, carry):
        r0 = pl.multiple_of(kbi * kb, kb)
        cblk = c_ref[pl.ds(r0, kb), :]
        ctblk = ct_ref[kbi]
        parts = []
        for cc in range(cpk):
            words = sel_ref[kbi * cpk + cc]
            for j in range(WORD_BITS):
                top = words if j == 0 else lax.shift_left(words, jnp.int32(j))
                parts.append(jnp.where(top < 0, 0.0, MASK_NEG))
        bias = jnp.concatenate(parts, axis=0)
        bias = jnp.concatenate([bias] * ATT_HEAD_GROUP, axis=1)
        for hg in range(heads // ATT_HEAD_GROUP):
            cols = slice(hg * gw, (hg + 1) * gw)
            s = jnp.dot(cblk, qlat_ref[:, cols], preferred_element_type=F32) + bias
            m_old = m_ref[:, cols]
            m_new = jnp.maximum(m_old, jnp.max(s, axis=0, keepdims=True))
            alpha = jnp.exp2((m_old - m_new) * cs)
            p = jnp.exp2((s - m_new) * cs)
            l_ref[:, cols] = l_ref[:, cols] * alpha + jnp.sum(p, axis=0, keepdims=True)
            pv = jnp.dot(ctblk, p.astype(ctblk.dtype), preferred_element_type=F32)
            acc_ref[:, cols] = acc_ref[:, cols] * alpha + pv
            m_ref[:, cols] = m_new
        return carry

    lax.fori_loop(0, n_kb, att_body, 0)

    inv_l = 1.0 / l_ref[...]
    for h in range(heads):
        cols = slice(h * qb, (h + 1) * qb)
        o_lat = (acc_ref[:, cols] * inv_l[:, cols]).astype(wuvt_ref.dtype)
        out_t = jnp.dot(wuvt_ref[h], o_lat, preferred_element_type=F32)
        o_ref[:, h * dh:(h + 1) * dh] = (out_t.T * gate_ref[:, h * dh:(h + 1) * dh]).astype(o_ref.dtype)


def _sparse_attention(q, qidx, wt, gate, c, ct, kidx, wuk, wuvt, *, batch, kb):
    m, aw = q.shape
    seq = m // batch
    heads, cdim, dh = wuk.shape
    iheads, di = wt.shape[0], kidx.shape[1]
    qb = V7X_LANES
    nq = seq // qb
    assert seq % kb == 0 and kb % CHUNK_ROWS == 0 and iheads % IDX_HEAD_GROUP == 0 and heads % ATT_HEAD_GROUP == 0
    topk = min(TOPK_MAX, seq // 4)
    qrow = lambda b, i: (b * nq + i, 0)
    resident = dict(pipeline_mode=pl.Buffered(1))
    return pl.pallas_call(
        functools.partial(_attn_kernel, qb=qb, kb=kb, topk=topk, sm_scale=dh ** -0.5, seq_bits=seq.bit_length()),
        grid=(batch, nq),
        in_specs=[pl.BlockSpec((qb, aw), qrow),
                  pl.BlockSpec((qb, iheads * di), qrow),
                  pl.BlockSpec((iheads, qb), lambda b, i: (0, b * nq + i)),
                  pl.BlockSpec((qb, aw), qrow),
                  pl.BlockSpec((seq, cdim), lambda b, i: (b, 0), **resident),
                  pl.BlockSpec((seq // kb, cdim, kb), lambda b, i: (b, 0, 0), **resident),
                  pl.BlockSpec((seq, di), lambda b, i: (b, 0), **resident),
                  pl.BlockSpec((heads, cdim, dh), lambda b, i: (0, 0, 0), **resident),
                  pl.BlockSpec((heads, dh, cdim), lambda b, i: (0, 0, 0), **resident)],
        out_specs=pl.BlockSpec((qb, aw), qrow),
        out_shape=jax.ShapeDtypeStruct((m, aw), BF16),
        scratch_shapes=[pltpu.VMEM((WORD_BITS, seq // CHUNK_ROWS, V7X_SUBLANES, qb), I32),
                        pltpu.VMEM((seq // CHUNK_ROWS, V7X_SUBLANES, qb), I32),
                        pltpu.VMEM((seq // CHUNK_ROWS, V7X_SUBLANES, qb), I32),
                        pltpu.VMEM((iheads * qb, di), BF16),
                        pltpu.VMEM((cdim, heads * qb), BF16),
                        pltpu.VMEM((cdim, heads * qb), F32),
                        pltpu.VMEM((1, heads * qb), F32),
                        pltpu.VMEM((1, heads * qb), F32)],
        compiler_params=_cparams("parallel", "arbitrary"),
        name="sparse_attention",
    )(q, qidx, wt, gate, c, ct, kidx, wuk, wuvt)


def _outproj_kernel(ya_ref, yb_ref, wa_ref, wb_ref, x_ref, o_ref):
    acc = jnp.dot(ya_ref[...], wa_ref[...], preferred_element_type=F32)
    acc = acc + jnp.dot(yb_ref[...], wb_ref[...], preferred_element_type=F32)
    o_ref[...] = x_ref[...] + acc


def _outproj(ya, yb, wa, wb, x):
    m, ka = ya.shape
    kbw = yb.shape[1]
    n = wa.shape[1]
    tm, tn = _tile(m, 1024), _tile(n, 512)
    return pl.pallas_call(
        _outproj_kernel,
        grid=(n // tn, m // tm),
        in_specs=[pl.BlockSpec((tm, ka), lambda j, i: (i, 0)), pl.BlockSpec((tm, kbw), lambda j, i: (i, 0)),
                  pl.BlockSpec((ka, tn), lambda j, i: (0, j)), pl.BlockSpec((kbw, tn), lambda j, i: (0, j)),
                  pl.BlockSpec((tm, tn), lambda j, i: (i, j))],
        out_specs=pl.BlockSpec((tm, tn), lambda j, i: (i, j)),
        out_shape=jax.ShapeDtypeStruct((m, n), F32),
        compiler_params=_cparams("parallel", "parallel"),
        name="outproj",
    )(ya, yb, wa, wb, x)


def kernel(x, norm_g, w_in, kv_norm_g, idx_k_norm_g, idx_k_norm_b, w_uk, w_uv, v_norm_g, v_norm_b, w_s, b_s,
           w_out, final_norm_g):
    batch, seq, d = x.shape
    depth = w_in.shape[0]
    cdim, heads, dh = w_uk.shape[1:]
    aw = heads * dh
    di = idx_k_norm_g.shape[1]
    bw = v_norm_g.shape[1]
    iheads = (w_in.shape[2] - 2 * aw - cdim - di - 3 * bw) // (di + 1)
    kb = _tile(seq, 512)

    names = ("q", "c_kv", "gate_a", "q_idx", "k_idx", "w_idx", "u", "v", "gate_b")
    sizes = (aw, cdim, aw, iheads * di, di, iheads, bw, bw, bw)
    seg, off = {}, 0
    for nm, sz in zip(names, sizes):
        seg[nm] = (off, off + sz)
        off += sz

    xf = x.reshape(batch * seq, d)
    for l in range(depth):
        w = {nm: w_in[l][:, a:b].astype(BF16) for nm, (a, b) in seg.items()}
        h = _rmsnorm(xf, norm_g[l], BF16)
        q = _proj(h, w["q"], "none", BF16, "proj_q")
        c, ct = _kv_latent(h, w["c_kv"], kv_norm_g[l], kb)
        gate_a = _proj(h, w["gate_a"], "silu", F32, "proj_gate_a")
        qidx = _proj(h, w["q_idx"], "none", BF16, "proj_q_idx")
        kidx, wt = _idx_small(h, w["k_idx"], w["w_idx"].T, idx_k_norm_g[l], idx_k_norm_b[l],
                              (iheads ** -0.5) * (di ** -0.5))
        wuk = jnp.transpose(w_uk[l], (1, 0, 2)).astype(BF16)
        wuvt = jnp.transpose(w_uv[l], (1, 2, 0)).astype(BF16)
        y_a = _sparse_attention(q, qidx, wt, gate_a, c, ct, kidx, wuk, wuvt, batch=batch, kb=kb)
        u = _proj(h, w["u"], "gelu", F32, "proj_u")
        v = _proj(h, w["v"], "gelu", F32, "proj_v")
        gate_b = _proj(h, w["gate_b"], "silu", F32, "proj_gate_b")
        y_b = _spatial(u, v, gate_b, w_s[l], b_s[l], v_norm_g[l], v_norm_b[l])
        wo = w_out[l].astype(BF16)
        xf = _outproj(y_a, y_b, wo[:aw], wo[aw:], xf)
    return _rmsnorm(xf, final_norm_g, x.dtype).reshape(batch, seq, d)
```

```python
import functools

import jax
import jax.numpy as jnp
from jax import lax
from jax.experimental import pallas as pl
from jax.experimental.pallas import tpu as pltpu

F32, BF16, I32 = jnp.float32, jnp.bfloat16, jnp.int32

EPS = 1e-6
TOPK_MAX = 256
SQRT_HALF = 0.7071067811865476
LOG2E = 1.4426950408889634

V7X_LANES = 128
V7X_SUBLANES = 8
V7X_VMEM_LIMIT_BYTES = 56 * 2**20

INT_MIN = -(2**31)
MASK_NEG = -1e30


def _cparams(*sem):
    return pltpu.CompilerParams(dimension_semantics=sem, vmem_limit_bytes=V7X_VMEM_LIMIT_BYTES)


def _tile(n, want):
    t = min(n, want)
    while n % t:
        t //= 2
    return t


def _rmsnorm_kernel(x_ref, g_ref, o_ref):
    x = x_ref[...]
    ms = jnp.mean(x * x, axis=-1, keepdims=True)
    o_ref[...] = (x * lax.rsqrt(ms + EPS) * g_ref[...]).astype(o_ref.dtype)


def _rmsnorm(x, g, out_dtype):
    m, d = x.shape
    tm = _tile(m, 256)
    return pl.pallas_call(
        _rmsnorm_kernel,
        grid=(m // tm,),
        in_specs=[pl.BlockSpec((tm, d), lambda i: (i, 0)), pl.BlockSpec((1, d), lambda i: (0, 0))],
        out_specs=pl.BlockSpec((tm, d), lambda i: (i, 0)),
        out_shape=jax.ShapeDtypeStruct((m, d), out_dtype),
        compiler_params=_cparams("parallel"),
        name="rmsnorm",
    )(x, g.reshape(1, d))


def _gelu(x):
    return 0.5 * x * (1.0 + lax.erf(x * SQRT_HALF))


_ACTS = {"none": lambda x: x, "silu": jax.nn.silu, "gelu": _gelu}


def _proj_kernel(h_ref, w_ref, o_ref, *, act):
    acc = jnp.dot(h_ref[...], w_ref[...], preferred_element_type=F32)
    o_ref[...] = _ACTS[act](acc).astype(o_ref.dtype)


def _proj(h, w, act, out_dtype, name):
    m, k = h.shape
    n = w.shape[1]
    tm, tn = _tile(m, 1024), _tile(n, 512)
    return pl.pallas_call(
        functools.partial(_proj_kernel, act=act),
        grid=(n // tn, m // tm),
        in_specs=[pl.BlockSpec((tm, k), lambda j, i: (i, 0)), pl.BlockSpec((k, tn), lambda j, i: (0, j))],
        out_specs=pl.BlockSpec((tm, tn), lambda j, i: (i, j)),
        out_shape=jax.ShapeDtypeStruct((m, n), out_dtype),
        compiler_params=_cparams("parallel", "parallel"),
        name=name,
    )(h, w)


def _kv_kernel(h_ref, w_ref, g_ref, c_ref, ct_ref, *, kb):
    c = jnp.dot(h_ref[...], w_ref[...], preferred_element_type=F32)
    ms = jnp.mean(c * c, axis=-1, keepdims=True)
    cn = c * lax.rsqrt(ms + EPS) * g_ref[...]
    c_ref[...] = cn.astype(c_ref.dtype)
    cnt = cn.T
    for j in range(ct_ref.shape[0]):
        ct_ref[j] = cnt[:, j * kb:(j + 1) * kb].astype(ct_ref.dtype)


def _kv_latent(h, w, g, kb):
    m, k = h.shape
    c = w.shape[1]
    tm = _tile(m, 512)
    assert tm % kb == 0
    return pl.pallas_call(
        functools.partial(_kv_kernel, kb=kb),
        grid=(m // tm,),
        in_specs=[pl.BlockSpec((tm, k), lambda i: (i, 0)), pl.BlockSpec((k, c), lambda i: (0, 0)),
                  pl.BlockSpec((1, c), lambda i: (0, 0))],
        out_specs=[pl.BlockSpec((tm, c), lambda i: (i, 0)), pl.BlockSpec((tm // kb, c, kb), lambda i: (i, 0, 0))],
        out_shape=[jax.ShapeDtypeStruct((m, c), BF16), jax.ShapeDtypeStruct((m // kb, c, kb), BF16)],
        compiler_params=_cparams("parallel"),
        name="kv_latent",
    )(h, w, g.reshape(1, c))


def _idx_kernel(h_ref, wk_ref, wwt_ref, g_ref, b_ref, k_ref, wt_ref, *, wscale):
    h = h_ref[...]
    k = jnp.dot(h, wk_ref[...], preferred_element_type=F32)
    mu = jnp.mean(k, axis=-1, keepdims=True)
    kc = k - mu
    var = jnp.mean(kc * kc, axis=-1, keepdims=True)
    k_ref[...] = (kc * lax.rsqrt(var + EPS) * g_ref[...] + b_ref[...]).astype(k_ref.dtype)
    wt = lax.dot_general(wwt_ref[...], h, (((1,), (1,)), ((), ())), preferred_element_type=F32)
    wt_ref[...] = wt * wscale


def _idx_small(h, wk, wwt, g, b, wscale):
    m, k = h.shape
    di, hi = wk.shape[1], wwt.shape[0]
    tm = _tile(m, 512)
    return pl.pallas_call(
        functools.partial(_idx_kernel, wscale=wscale),
        grid=(m // tm,),
        in_specs=[pl.BlockSpec((tm, k), lambda i: (i, 0)), pl.BlockSpec((k, di), lambda i: (0, 0)),
                  pl.BlockSpec((hi, k), lambda i: (0, 0)), pl.BlockSpec((1, di), lambda i: (0, 0)),
                  pl.BlockSpec((1, di), lambda i: (0, 0))],
        out_specs=[pl.BlockSpec((tm, di), lambda i: (i, 0)), pl.BlockSpec((hi, tm), lambda i: (0, i))],
        out_shape=[jax.ShapeDtypeStruct((m, di), BF16), jax.ShapeDtypeStruct((hi, m), F32)],
        compiler_params=_cparams("parallel"),
        name="idx_small",
    )(h, wk, wwt, g.reshape(1, di), b.reshape(1, di))


def _spatial_kernel(u_ref, v_ref, gb_ref, ws_ref, bst_ref, vg_ref, vb_ref, o_ref, *, chunk, groups):
    tm, bw = v_ref.shape
    e = bw // groups
    v = v_ref[...]
    mu = jnp.mean(v, axis=-1, keepdims=True)
    vc = v - mu
    var = jnp.mean(vc * vc, axis=-1, keepdims=True)
    vn = (vc * lax.rsqrt(var + EPS) * vg_ref[...] + vb_ref[...]).astype(BF16)
    row = lax.broadcasted_iota(I32, (chunk, chunk), 0)
    col = lax.broadcasted_iota(I32, (chunk, chunk), 1)
    causal = col <= row
    for g in range(groups):
        wsg = jnp.where(causal, ws_ref[g], 0.0).astype(BF16)
        bias = jnp.broadcast_to(bst_ref[:, g:g + 1], (chunk, e))
        for c in range(tm // chunk):
            rows, cols = slice(c * chunk, (c + 1) * chunk), slice(g * e, (g + 1) * e)
            mixed = jnp.dot(wsg, vn[rows, cols], preferred_element_type=F32) + bias
            o_ref[rows, cols] = (u_ref[rows, cols] * mixed * gb_ref[rows, cols]).astype(o_ref.dtype)


def _spatial(u, v, gate, w_s, b_s, vg, vb):
    m, bw = u.shape
    groups, chunk, _ = w_s.shape
    tm = 2 * chunk if m % (2 * chunk) == 0 else chunk
    row = lambda i: (i, 0)
    return pl.pallas_call(
        functools.partial(_spatial_kernel, chunk=chunk, groups=groups),
        grid=(m // tm,),
        in_specs=[pl.BlockSpec((tm, bw), row), pl.BlockSpec((tm, bw), row), pl.BlockSpec((tm, bw), row),
                  pl.BlockSpec((groups, chunk, chunk), lambda i: (0, 0, 0)),
                  pl.BlockSpec((chunk, groups), lambda i: (0, 0)),
                  pl.BlockSpec((1, bw), lambda i: (0, 0)), pl.BlockSpec((1, bw), lambda i: (0, 0))],
        out_specs=pl.BlockSpec((tm, bw), row),
        out_shape=jax.ShapeDtypeStruct((m, bw), BF16),
        compiler_params=_cparams("parallel"),
        name="spatial_gating",
    )(u, v, gate, w_s, b_s.T, vg.reshape(1, bw), vb.reshape(1, bw))


IDX_HEAD_GROUP = 4
ATT_HEAD_GROUP = 16
WORD_BITS = 32
CHUNK_ROWS = WORD_BITS * V7X_SUBLANES


def _bit_transpose32(a):
    a = list(a)
    m, j = 0x0000FFFF, 16
    while j:
        k = 0
        while k < WORD_BITS:
            t = (a[k] ^ lax.shift_right_logical(a[k + j], jnp.int32(j))) & jnp.int32(m)
            a[k] = a[k] ^ t
            a[k + j] = a[k + j] ^ lax.shift_left(t, jnp.int32(j))
            k = (k + j + 1) & ~j
        j >>= 1
        m = (m ^ (m << j)) & 0xFFFFFFFF
    return a


def _attn_kernel(q_ref, qidx_ref, wt_ref, gate_ref, c_ref, ct_ref, kidx_ref, wuk_ref, wuvt_ref, o_ref,
                 planes_ref, alive_ref, sel_ref, qs_ref, qlat_ref, acc_ref, m_ref, l_ref,
                 *, qb, kb, topk, sm_scale, seq_bits):
    heads, cdim, dh = wuk_ref.shape
    iheads, di = wt_ref.shape[0], kidx_ref.shape[1]
    cpk = kb // CHUNK_ROWS
    i = pl.program_id(1)
    n_kb = ((i + 1) * qb + kb - 1) // kb
    nt = (((1,), (1,)), ((), ()))

    for h in range(iheads):
        qs_ref[h * qb:(h + 1) * qb, :] = qidx_ref[:, h * di:(h + 1) * di]
    for h in range(heads):
        ql = lax.dot_general(wuk_ref[h], q_ref[:, h * dh:(h + 1) * dh], nt, preferred_element_type=F32)
        qlat_ref[:, h * qb:(h + 1) * qb] = ql.astype(qlat_ref.dtype)

    wt = wt_ref[...]
    qpos = i * qb + lax.broadcasted_iota(I32, (1, qb), 1)

    @pl.when(i == 0)
    def _():
        planes_ref[...] = jnp.zeros(planes_ref.shape, I32)

    def score_block(kbi, masked):
        r0 = pl.multiple_of(kbi * kb, kb)
        kblk = kidx_ref[pl.ds(r0, kb), :]
        score = jnp.zeros((kb, qb), F32)
        for hg in range(iheads // IDX_HEAD_GROUP):
            h0 = hg * IDX_HEAD_GROUP
            lt = lax.dot_general(kblk, qs_ref[h0 * qb:(h0 + IDX_HEAD_GROUP) * qb, :], nt,
                                 preferred_element_type=F32)
            for hh in range(IDX_HEAD_GROUP):
                score = score + jnp.maximum(lt[:, hh * qb:(hh + 1) * qb], 0.0) * wt[h0 + hh:h0 + hh + 1, :]
        bits = pltpu.bitcast(score, I32)
        key = bits ^ ((bits >> 31) | INT_MIN)
        key = jnp.where(key == 0x7FFFFFFF, INT_MIN, key)
        if masked:
            kpos = r0 + lax.broadcasted_iota(I32, (kb, 1), 0)
            key = jnp.where(kpos <= qpos, key, 0)
        for cc in range(cpk):
            rows = [key[cc * CHUNK_ROWS + j * V7X_SUBLANES:cc * CHUNK_ROWS + (j + 1) * V7X_SUBLANES, :]
                    for j in range(WORD_BITS)]
            planes = _bit_transpose32(rows)
            for b in range(WORD_BITS):
                planes_ref[b, kbi * cpk + cc] = planes[b]

    def score_body(kbi, carry):
        score_block(kbi, False)
        return carry

    lax.fori_loop(0, n_kb - 1, score_body, 0)
    score_block(n_kb - 1, True)

    def popsum(words):
        return jnp.sum(jnp.sum(lax.population_count(words), axis=0), axis=0, keepdims=True)

    cidx = lax.broadcasted_iota(I32, alive_ref.shape, 0)
    alive_ref[...] = jnp.where(cidx < n_kb * cpk, -1, 0)
    sel_ref[...] = jnp.zeros(sel_ref.shape, I32)

    def radix_body(b, k_rem):
        plane = planes_ref[b]
        alive = alive_ref[...]
        ones = alive & plane
        cnt = popsum(ones)
        take1 = cnt >= k_rem
        alive_ref[...] = jnp.where(take1, ones, alive & ~plane)
        sel_ref[...] = jnp.where(take1, sel_ref[...], sel_ref[...] | ones)
        return jnp.where(take1, k_rem, k_rem - cnt)

    need = lax.fori_loop(0, WORD_BITS, radix_body, jnp.minimum(qpos + 1, topk))

    @pl.when(jnp.max(popsum(alive_ref[...]) - need) > 0)
    def _():
        ties = alive_ref[...]
        row0 = cidx * CHUNK_ROWS + lax.broadcasted_iota(I32, alive_ref.shape, 1)

        def rows_below(lim):
            nj = jnp.clip(lax.shift_right_arithmetic(lim - row0 + (V7X_SUBLANES - 1), 3), 0, WORD_BITS)
            return jnp.where(nj <= 0, 0, lax.shift_left(jnp.int32(-1), WORD_BITS - jnp.maximum(nj, 1)))

        def idx_body(b, lim):
            cand = lim + lax.shift_left(jnp.int32(1), seq_bits - 1 - b)
            return jnp.where(popsum(ties & rows_below(cand)) < need, cand, lim)

        lim = lax.fori_loop(0, seq_bits, idx_body, jnp.zeros((1, qb), I32))
        alive_ref[...] = ties & rows_below(lim + 1)

    sel_ref[...] = sel_ref[...] | alive_ref[...]

    m_ref[...] = jnp.full(m_ref.shape, MASK_NEG, F32)
    l_ref[...] = jnp.zeros(l_ref.shape, F32)
    acc_ref[...] = jnp.zeros(acc_ref.shape, F32)
    cs = sm_scale * LOG2E
    gw = ATT_HEAD_GROUP * qb

    def att_body(kbi, carry):
        r0 = pl.multiple_of(kbi * kb, kb)
        cblk = c_ref[pl.ds(r0, kb), :]
        ctblk = ct_ref[kbi]
        parts = []
        for cc in range(cpk):
            words = sel_ref[kbi * cpk + cc]
            for j in range(WORD_BITS):
                top = words if j == 0 else lax.shift_left(words, jnp.int32(j))
                parts.append(jnp.where(top < 0, 0.0, MASK_NEG))
        bias = jnp.concatenate(parts, axis=0)
        bias = jnp.concatenate([bias] * ATT_HEAD_GROUP, axis=1)
        for hg in range(heads // ATT_HEAD_GROUP):
            cols = slice(hg * gw, (hg + 1) * gw)
            s = jnp.dot(cblk, qlat_ref[:, cols], preferred_element_type=F32) + bias
            m_old = m_ref[:, cols]
            m_new = jnp.maximum(m_old, jnp.max(s, axis=0, keepdims=True))
            alpha = jnp.exp2((m_old - m_new) * cs)
            p = jnp.exp2((s - m_new) * cs)
            l_ref[:, cols] = l_ref[:, cols] * alpha + jnp.sum(p, axis=0, keepdims=True)
            pv = jnp.dot(ctblk, p.astype(ctblk.dtype), preferred_element_type=F32)
            acc_ref[:, cols] = acc_ref[:, cols] * alpha + pv
            m_ref[:, cols] = m_new
        return carry

    lax.fori_loop(0, n_kb, att_body, 0)

    inv_l = 1.0 / l_ref[...]
    for h in range(heads):
        cols = slice(h * qb, (h + 1) * qb)
        o_lat = (acc_ref[:, cols] * inv_l[:, cols]).astype(wuvt_ref.dtype)
        out_t = jnp.dot(wuvt_ref[h], o_lat, preferred_element_type=F32)
        o_ref[:, h * dh:(h + 1) * dh] = (out_t.T * gate_ref[:, h * dh:(h + 1) * dh]).astype(o_ref.dtype)


def _sparse_attention(q, qidx, wt, gate, c, ct, kidx, wuk, wuvt, *, batch, kb):
    m, aw = q.shape
    seq = m // batch
    heads, cdim, dh = wuk.shape
    iheads, di = wt.shape[0], kidx.shape[1]
    qb = V7X_LANES
    nq = seq // qb
    assert seq % kb == 0 and kb % CHUNK_ROWS == 0 and iheads % IDX_HEAD_GROUP == 0 and heads % ATT_HEAD_GROUP == 0
    topk = min(TOPK_MAX, seq // 4)
    qrow = lambda b, i: (b * nq + i, 0)
    resident = dict(pipeline_mode=pl.Buffered(1))
    return pl.pallas_call(
        functools.partial(_attn_kernel, qb=qb, kb=kb, topk=topk, sm_scale=dh ** -0.5, seq_bits=seq.bit_length()),
        grid=(batch, nq),
        in_specs=[pl.BlockSpec((qb, aw), qrow),
                  pl.BlockSpec((qb, iheads * di), qrow),
                  pl.BlockSpec((iheads, qb), lambda b, i: (0, b * nq + i)),
                  pl.BlockSpec((qb, aw), qrow),
                  pl.BlockSpec((seq, cdim), lambda b, i: (b, 0), **resident),
                  pl.BlockSpec((seq // kb, cdim, kb), lambda b, i: (b, 0, 0), **resident),
                  pl.BlockSpec((seq, di), lambda b, i: (b, 0), **resident),
                  pl.BlockSpec((heads, cdim, dh), lambda b, i: (0, 0, 0), **resident),
                  pl.BlockSpec((heads, dh, cdim), lambda b, i: (0, 0, 0), **resident)],
        out_specs=pl.BlockSpec((qb, aw), qrow),
        out_shape=jax.ShapeDtypeStruct((m, aw), BF16),
        scratch_shapes=[pltpu.VMEM((WORD_BITS, seq // CHUNK_ROWS, V7X_SUBLANES, qb), I32),
                        pltpu.VMEM((seq // CHUNK_ROWS, V7X_SUBLANES, qb), I32),
                        pltpu.VMEM((seq // CHUNK_ROWS, V7X_SUBLANES, qb), I32),
                        pltpu.VMEM((iheads * qb, di), BF16),
                        pltpu.VMEM((cdim, heads * qb), BF16),
                        pltpu.VMEM((cdim, heads * qb), F32),
                        pltpu.VMEM((1, heads * qb), F32),
                        pltpu.VMEM((1, heads * qb), F32)],
        compiler_params=_cparams("parallel", "arbitrary"),
        name="sparse_attention",
    )(q, qidx, wt, gate, c, ct, kidx, wuk, wuvt)


def _outproj_kernel(ya_ref, yb_ref, wa_ref, wb_ref, x_ref, o_ref):
    acc = jnp.dot(ya_ref[...], wa_ref[...], preferred_element_type=F32)
    acc = acc + jnp.dot(yb_ref[...], wb_ref[...], preferred_element_type=F32)
    o_ref[...] = x_ref[...] + acc


def _outproj(ya, yb, wa, wb, x):
    m, ka = ya.shape
    kbw = yb.shape[1]
    n = wa.shape[1]
    tm, tn = _tile(m, 1024), _tile(n, 512)
    return pl.pallas_call(
        _outproj_kernel,
        grid=(n // tn, m // tm),
        in_specs=[pl.BlockSpec((tm, ka), lambda j, i: (i, 0)), pl.BlockSpec((tm, kbw), lambda j, i: (i, 0)),
                  pl.BlockSpec((ka, tn), lambda j, i: (0, j)), pl.BlockSpec((kbw, tn), lambda j, i: (0, j)),
                  pl.BlockSpec((tm, tn), lambda j, i: (i, j))],
        out_specs=pl.BlockSpec((tm, tn), lambda j, i: (i, j)),
        out_shape=jax.ShapeDtypeStruct((m, n), F32),
        compiler_params=_cparams("parallel", "parallel"),
        name="outproj",
    )(ya, yb, wa, wb, x)


def kernel(x, norm_g, w_in, kv_norm_g, idx_k_norm_g, idx_k_norm_b, w_uk, w_uv, v_norm_g, v_norm_b, w_s, b_s,
           w_out, final_norm_g):
    batch, seq, d = x.shape
    depth = w_in.shape[0]
    cdim, heads, dh = w_uk.shape[1:]
    aw = heads * dh
    di = idx_k_norm_g.shape[1]
    bw = v_norm_g.shape[1]
    iheads = (w_in.shape[2] - 2 * aw - cdim - di - 3 * bw) // (di + 1)
    kb = _tile(seq, 512)

    names = ("q", "c_kv", "gate_a", "q_idx", "k_idx", "w_idx", "u", "v", "gate_b")
    sizes = (aw, cdim, aw, iheads * di, di, iheads, bw, bw, bw)
    seg, off = {}, 0
    for nm, sz in zip(names, sizes):
        seg[nm] = (off, off + sz)
        off += sz

    xf = x.reshape(batch * seq, d)
    for l in range(depth):
        w = {nm: w_in[l][:, a:b].astype(BF16) for nm, (a, b) in seg.items()}
        h = _rmsnorm(xf, norm_g[l], BF16)
        q = _proj(h, w["q"], "none", BF16, "proj_q")
        c, ct = _kv_latent(h, w["c_kv"], kv_norm_g[l], kb)
        gate_a = _proj(h, w["gate_a"], "silu", F32, "proj_gate_a")
        qidx = _proj(h, w["q_idx"], "none", BF16, "proj_q_idx")
        kidx, wt = _idx_small(h, w["k_idx"], w["w_idx"].T, idx_k_norm_g[l], idx_k_norm_b[l],
                              (iheads ** -0.5) * (di ** -0.5))
        wuk = jnp.transpose(w_uk[l], (1, 0, 2)).astype(BF16)
        wuvt = jnp.transpose(w_uv[l], (1, 2, 0)).astype(BF16)
        y_a = _sparse_attention(q, qidx, wt, gate_a, c, ct, kidx, wuk, wuvt, batch=batch, kb=kb)
        u = _proj(h, w["u"], "gelu", F32, "proj_u")
        v = _proj(h, w["v"], "gelu", F32, "proj_v")
        gate_b = _proj(h, w["gate_b"], "silu", F32, "proj_gate_b")
        y_b = _spatial(u, v, gate_b, w_s[l], b_s[l], v_norm_g[l], v_norm_b[l])
        wo = w_out[l].astype(BF16)
        xf = _outproj(y_a, y_b, wo[:aw], wo[aw:], xf)
    return _rmsnorm(xf, final_norm_g, x.dtype).reshape(batch, seq, d)
```

```python
import functools

import jax
import jax.numpy as jnp
from jax import lax
from jax.experimental import pallas as pl
from jax.experimental.pallas import tpu as pltpu

F32, BF16, I32 = jnp.float32, jnp.bfloat16, jnp.int32

EPS = 1e-6
TOPK_MAX = 256
SQRT_HALF = 0.7071067811865476
LOG2E = 1.4426950408889634

V7X_LANES = 128
V7X_SUBLANES = 8
V7X_VMEM_LIMIT_BYTES = 56 * 2**20

INT_MIN = -(2**31)
MASK_NEG = -1e30


def _cparams(*sem):
    return pltpu.CompilerParams(dimension_semantics=sem, vmem_limit_bytes=V7X_VMEM_LIMIT_BYTES)


def _tile(n, want):
    t = min(n, want)
    while n % t:
        t //= 2
    return t


def _rmsnorm_kernel(x_ref, g_ref, o_ref):
    x = x_ref[...]
    ms = jnp.mean(x * x, axis=-1, keepdims=True)
    o_ref[...] = (x * lax.rsqrt(ms + EPS) * g_ref[...]).astype(o_ref.dtype)


def _rmsnorm(x, g, out_dtype):
    m, d = x.shape
    tm = _tile(m, 256)
    return pl.pallas_call(
        _rmsnorm_kernel,
        grid=(m // tm,),
        in_specs=[pl.BlockSpec((tm, d), lambda i: (i, 0)), pl.BlockSpec((1, d), lambda i: (0, 0))],
        out_specs=pl.BlockSpec((tm, d), lambda i: (i, 0)),
        out_shape=jax.ShapeDtypeStruct((m, d), out_dtype),
        compiler_params=_cparams("parallel"),
        name="rmsnorm",
    )(x, g.reshape(1, d))


def _gelu(x):
    return 0.5 * x * (1.0 + lax.erf(x * SQRT_HALF))


_ACTS = {"none": lambda x: x, "silu": jax.nn.silu, "gelu": _gelu}


def _proj_kernel(h_ref, w_ref, o_ref, *, act, scale):
    acc = jnp.dot(h_ref[...], w_ref[...], preferred_element_type=F32)
    if scale is not None:
        acc = acc * scale
    o_ref[...] = _ACTS[act](acc).astype(o_ref.dtype)


def _proj(h, w, act, out_dtype, name, scale=None):
    m, k = h.shape
    n = w.shape[1]
    tm, tn = _tile(m, 1024), _tile(n, 512)
    return pl.pallas_call(
        functools.partial(_proj_kernel, act=act, scale=scale),
        grid=(n // tn, m // tm),
        in_specs=[pl.BlockSpec((tm, k), lambda j, i: (i, 0)), pl.BlockSpec((k, tn), lambda j, i: (0, j))],
        out_specs=pl.BlockSpec((tm, tn), lambda j, i: (i, j)),
        out_shape=jax.ShapeDtypeStruct((m, n), out_dtype),
        compiler_params=_cparams("parallel", "parallel"),
        name=name,
    )(h, w)


def _kv_kernel(h_ref, w_ref, g_ref, c_ref):
    c = jnp.dot(h_ref[...], w_ref[...], preferred_element_type=F32)
    ms = jnp.mean(c * c, axis=-1, keepdims=True)
    c_ref[...] = (c * lax.rsqrt(ms + EPS) * g_ref[...]).astype(c_ref.dtype)


def _kv_latent(h, w, g):
    m, k = h.shape
    c = w.shape[1]
    tm = _tile(m, 512)
    return pl.pallas_call(
        _kv_kernel,
        grid=(m // tm,),
        in_specs=[pl.BlockSpec((tm, k), lambda i: (i, 0)), pl.BlockSpec((k, c), lambda i: (0, 0)),
                  pl.BlockSpec((1, c), lambda i: (0, 0))],
        out_specs=pl.BlockSpec((tm, c), lambda i: (i, 0)),
        out_shape=jax.ShapeDtypeStruct((m, c), BF16),
        compiler_params=_cparams("parallel"),
        name="kv_latent",
    )(h, w, g.reshape(1, c))


def _idx_kernel(h_ref, wk_ref, wwt_ref, g_ref, b_ref, k_ref, wt_ref, *, wscale):
    h = h_ref[...]
    k = jnp.dot(h, wk_ref[...], preferred_element_type=F32)
    mu = jnp.mean(k, axis=-1, keepdims=True)
    kc = k - mu
    var = jnp.mean(kc * kc, axis=-1, keepdims=True)
    k_ref[...] = (kc * lax.rsqrt(var + EPS) * g_ref[...] + b_ref[...]).astype(k_ref.dtype)
    wt = lax.dot_general(wwt_ref[...], h, (((1,), (1,)), ((), ())), preferred_element_type=F32)
    wt_ref[...] = wt * wscale


def _idx_small(h, wk, wwt, g, b, wscale):
    m, k = h.shape
    di, hi = wk.shape[1], wwt.shape[0]
    tm = _tile(m, 512)
    return pl.pallas_call(
        functools.partial(_idx_kernel, wscale=wscale),
        grid=(m // tm,),
        in_specs=[pl.BlockSpec((tm, k), lambda i: (i, 0)), pl.BlockSpec((k, di), lambda i: (0, 0)),
                  pl.BlockSpec((hi, k), lambda i: (0, 0)), pl.BlockSpec((1, di), lambda i: (0, 0)),
                  pl.BlockSpec((1, di), lambda i: (0, 0))],
        out_specs=[pl.BlockSpec((tm, di), lambda i: (i, 0)), pl.BlockSpec((hi, tm), lambda i: (0, i))],
        out_shape=[jax.ShapeDtypeStruct((m, di), BF16), jax.ShapeDtypeStruct((hi, m), F32)],
        compiler_params=_cparams("parallel"),
        name="idx_small",
    )(h, wk, wwt, g.reshape(1, di), b.reshape(1, di))


def _spatial_kernel(u_ref, v_ref, gb_ref, ws_ref, bst_ref, vg_ref, vb_ref, o_ref, *, chunk, groups):
    tm, bw = v_ref.shape
    e = bw // groups
    v = v_ref[...]
    mu = jnp.mean(v, axis=-1, keepdims=True)
    vc = v - mu
    var = jnp.mean(vc * vc, axis=-1, keepdims=True)
    vn = (vc * lax.rsqrt(var + EPS) * vg_ref[...] + vb_ref[...]).astype(BF16)
    row = lax.broadcasted_iota(I32, (chunk, chunk), 0)
    col = lax.broadcasted_iota(I32, (chunk, chunk), 1)
    causal = col <= row
    for g in range(groups):
        wsg = jnp.where(causal, ws_ref[g], 0.0).astype(BF16)
        bias = jnp.broadcast_to(bst_ref[:, g:g + 1], (chunk, e))
        for c in range(tm // chunk):
            rows, cols = slice(c * chunk, (c + 1) * chunk), slice(g * e, (g + 1) * e)
            mixed = jnp.dot(wsg, vn[rows, cols], preferred_element_type=F32) + bias
            o_ref[rows, cols] = (u_ref[rows, cols] * mixed * gb_ref[rows, cols]).astype(o_ref.dtype)


def _spatial(u, v, gate, w_s, b_s, vg, vb):
    m, bw = u.shape
    groups, chunk, _ = w_s.shape
    tm = 2 * chunk if m % (2 * chunk) == 0 else chunk
    row = lambda i: (i, 0)
    return pl.pallas_call(
        functools.partial(_spatial_kernel, chunk=chunk, groups=groups),
        grid=(m // tm,),
        in_specs=[pl.BlockSpec((tm, bw), row), pl.BlockSpec((tm, bw), row), pl.BlockSpec((tm, bw), row),
                  pl.BlockSpec((groups, chunk, chunk), lambda i: (0, 0, 0)),
                  pl.BlockSpec((chunk, groups), lambda i: (0, 0)),
                  pl.BlockSpec((1, bw), lambda i: (0, 0)), pl.BlockSpec((1, bw), lambda i: (0, 0))],
        out_specs=pl.BlockSpec((tm, bw), row),
        out_shape=jax.ShapeDtypeStruct((m, bw), BF16),
        compiler_params=_cparams("parallel"),
        name="spatial_gating",
    )(u, v, gate, w_s, b_s.T, vg.reshape(1, bw), vb.reshape(1, bw))


IDX_HEAD_GROUP = 4
WORD_BITS = 32
ONES_ROWS = 16
CHUNK_ROWS = WORD_BITS * V7X_SUBLANES


def _bit_transpose32(a):
    a = list(a)
    m, j = 0x0000FFFF, 16
    while j:
        k = 0
        while k < WORD_BITS:
            t = (a[k] ^ lax.shift_right_logical(a[k + j], jnp.int32(j))) & jnp.int32(m)
            a[k] = a[k] ^ t
            a[k + j] = a[k + j] ^ lax.shift_left(t, jnp.int32(j))
            k = (k + j + 1) & ~j
        j >>= 1
        m = (m ^ (m << j)) & 0xFFFFFFFF
    return a


def _attn_kernel(qi_ref, ki_ref, q_ref, qidx_ref, wt_ref, gate_ref, kidx_ref, kp_ref, vt_ref, o_ref,
                 planes_ref, alive_ref, sel_ref, qs_ref, bias_ref, acc_ref, m_ref,
                 *, qb, kb, topk, seq_bits):
    t = pl.program_id(1)
    i, kbi = qi_ref[t], ki_ref[t]
    heads, dh = vt_ref.shape[1], vt_ref.shape[2] - ONES_ROWS
    iheads, di = wt_ref.shape[0], kidx_ref.shape[1]
    cpk = kb // CHUNK_ROWS
    n_kb = ((i + 1) * qb + kb - 1) // kb
    nt = (((1,), (1,)), ((), ()))

    @pl.when(kbi == 0)
    def _select():
        for h in range(iheads):
            qs_ref[h * qb:(h + 1) * qb, :] = qidx_ref[:, h * di:(h + 1) * di]
        wt = wt_ref[...]
        qpos = i * qb + lax.broadcasted_iota(I32, (1, qb), 1)

        @pl.when(i == 0)
        def _():
            planes_ref[...] = jnp.zeros(planes_ref.shape, I32)

        def score_block(kbj, masked):
            r0 = pl.multiple_of(kbj * kb, kb)
            kblk = kidx_ref[pl.ds(r0, kb), :]
            score = jnp.zeros((kb, qb), F32)
            for hg in range(iheads // IDX_HEAD_GROUP):
                h0 = hg * IDX_HEAD_GROUP
                lt = lax.dot_general(kblk, qs_ref[h0 * qb:(h0 + IDX_HEAD_GROUP) * qb, :], nt,
                                     preferred_element_type=F32)
                for hh in range(IDX_HEAD_GROUP):
                    score = score + jnp.maximum(lt[:, hh * qb:(hh + 1) * qb], 0.0) * wt[h0 + hh:h0 + hh + 1, :]
            bits = pltpu.bitcast(score, I32)
            key = bits ^ ((bits >> 31) | INT_MIN)
            key = jnp.where(key == 0x7FFFFFFF, INT_MIN, key)
            if masked:
                kpos = r0 + lax.broadcasted_iota(I32, (kb, 1), 0)
                key = jnp.where(kpos <= qpos, key, 0)
            for cc in range(cpk):
                rows = [key[cc * CHUNK_ROWS + j * V7X_SUBLANES:cc * CHUNK_ROWS + (j + 1) * V7X_SUBLANES, :]
                        for j in range(WORD_BITS)]
                planes = _bit_transpose32(rows)
                for b in range(WORD_BITS):
                    planes_ref[b, kbj * cpk + cc] = planes[b]

        def score_body(kbj, carry):
            score_block(kbj, False)
            return carry

        lax.fori_loop(0, n_kb - 1, score_body, 0)
        score_block(n_kb - 1, True)

        def popsum(words):
            return jnp.sum(jnp.sum(lax.population_count(words), axis=0), axis=0, keepdims=True)

        cidx = lax.broadcasted_iota(I32, alive_ref.shape, 0)
        alive_ref[...] = jnp.where(cidx < n_kb * cpk, -1, 0)
        sel_ref[...] = jnp.zeros(sel_ref.shape, I32)

        def radix_body(b, k_rem):
            plane = planes_ref[b]
            alive = alive_ref[...]
            ones = alive & plane
            cnt = popsum(ones)
            take1 = cnt >= k_rem
            alive_ref[...] = jnp.where(take1, ones, alive & ~plane)
            sel_ref[...] = jnp.where(take1, sel_ref[...], sel_ref[...] | ones)
            return jnp.where(take1, k_rem, k_rem - cnt)

        need = lax.fori_loop(0, WORD_BITS, radix_body, jnp.minimum(qpos + 1, topk))

        @pl.when(jnp.max(popsum(alive_ref[...]) - need) > 0)
        def _():
            ties = alive_ref[...]
            row0 = cidx * CHUNK_ROWS + lax.broadcasted_iota(I32, alive_ref.shape, 1)

            def rows_below(lim):
                nj = jnp.clip(lax.shift_right_arithmetic(lim - row0 + (V7X_SUBLANES - 1), 3), 0, WORD_BITS)
                return jnp.where(nj <= 0, 0, lax.shift_left(jnp.int32(-1), WORD_BITS - jnp.maximum(nj, 1)))

            def idx_body(b, lim):
                cand = lim + lax.shift_left(jnp.int32(1), seq_bits - 1 - b)
                return jnp.where(popsum(ties & rows_below(cand)) < need, cand, lim)

            lim = lax.fori_loop(0, seq_bits, idx_body, jnp.zeros((1, qb), I32))
            alive_ref[...] = ties & rows_below(lim + 1)

        sel_ref[...] = sel_ref[...] | alive_ref[...]

        def bias_body(c, carry):
            words = sel_ref[c]
            for j in range(WORD_BITS):
                top = words if j == 0 else lax.shift_left(words, jnp.int32(j))
                r = pl.multiple_of(c * CHUNK_ROWS, CHUNK_ROWS) + j * V7X_SUBLANES
                bias_ref[pl.ds(r, V7X_SUBLANES), :] = jnp.where(top < 0, 0.0, MASK_NEG)
            return carry

        lax.fori_loop(0, n_kb * cpk, bias_body, 0)
        m_ref[...] = jnp.full(m_ref.shape, MASK_NEG, F32)
        acc_ref[...] = jnp.zeros(acc_ref.shape, F32)

    r0 = pl.multiple_of(kbi * kb, kb)
    bias = bias_ref[pl.ds(r0, kb), :]
    s = jnp.concatenate(
        [lax.dot_general(kp_ref[:, h * dh:(h + 1) * dh], q_ref[:, h * dh:(h + 1) * dh], nt,
                         preferred_element_type=F32) + bias for h in range(heads)], axis=1)
    m_old = m_ref[...]
    m_new = jnp.maximum(m_old, jnp.max(s, axis=0, keepdims=True))
    alpha = jnp.exp2(m_old - m_new)
    p = jnp.exp2(s - m_new).astype(vt_ref.dtype)
    m_ref[...] = m_new
    for h in range(heads):
        qc = slice(h * qb, (h + 1) * qb)
        pv = jnp.dot(vt_ref[0, h], p[:, qc], preferred_element_type=F32)
        acc_ref[h] = acc_ref[h] * alpha[:, qc] + pv

    @pl.when(kbi == n_kb - 1)
    def _finish():
        for h in range(heads):
            hc = slice(h * dh, (h + 1) * dh)
            o_t = acc_ref[h, :dh, :] * (1.0 / acc_ref[h, dh:dh + 1, :])
            o_ref[:, hc] = (o_t.T * gate_ref[:, hc]).astype(o_ref.dtype)


def _causal_steps(nq, qb, kb):
    qi, ki = [], []
    for i in range(nq):
        for k in range(((i + 1) * qb + kb - 1) // kb):
            qi.append(i)
            ki.append(k)
    return jnp.asarray(qi, I32), jnp.asarray(ki, I32)


def _sparse_attention(q, qidx, wt, gate, kidx, kproj, vt, *, batch, kb):
    m, aw = q.shape
    seq = m // batch
    heads, dh = vt.shape[1], vt.shape[2] - ONES_ROWS
    iheads, di = wt.shape[0], kidx.shape[1]
    qb = min(2 * V7X_LANES, seq)
    nq, nkb = seq // qb, seq // kb
    assert seq % kb == 0 and kb % CHUNK_ROWS == 0 and iheads % IDX_HEAD_GROUP == 0
    topk = min(TOPK_MAX, seq // 4)
    qi, ki = _causal_steps(nq, qb, kb)
    qrow = lambda b, t, qi, ki: (b * nq + qi[t], 0)
    nchunks = seq // CHUNK_ROWS
    grid_spec = pltpu.PrefetchScalarGridSpec(
        num_scalar_prefetch=2,
        grid=(batch, qi.shape[0]),
        in_specs=[pl.BlockSpec((qb, aw), qrow),
                  pl.BlockSpec((qb, iheads * di), qrow),
                  pl.BlockSpec((iheads, qb), lambda b, t, qi, ki: (0, b * nq + qi[t])),
                  pl.BlockSpec((qb, aw), qrow),
                  pl.BlockSpec((seq, di), lambda b, t, qi, ki: (b, 0), pipeline_mode=pl.Buffered(1)),
                  pl.BlockSpec((kb, aw), lambda b, t, qi, ki: (b * nkb + ki[t], 0)),
                  pl.BlockSpec((1, heads, dh + ONES_ROWS, kb), lambda b, t, qi, ki: (b * nkb + ki[t], 0, 0, 0))],
        out_specs=pl.BlockSpec((qb, aw), qrow),
        scratch_shapes=[pltpu.VMEM((WORD_BITS, nchunks, V7X_SUBLANES, qb), I32),
                        pltpu.VMEM((nchunks, V7X_SUBLANES, qb), I32),
                        pltpu.VMEM((nchunks, V7X_SUBLANES, qb), I32),
                        pltpu.VMEM((iheads * qb, di), BF16),
                        pltpu.VMEM((seq, qb), F32),
                        pltpu.VMEM((heads, dh + ONES_ROWS, qb), F32),
                        pltpu.VMEM((1, heads * qb), F32)])
    return pl.pallas_call(
        functools.partial(_attn_kernel, qb=qb, kb=kb, topk=topk, seq_bits=seq.bit_length()),
        grid_spec=grid_spec,
        out_shape=jax.ShapeDtypeStruct((m, aw), BF16),
        compiler_params=_cparams("parallel", "arbitrary"),
        name="sparse_attention",
    )(qi, ki, q, qidx, wt, gate, kidx, kproj, vt)


def _vproj_t_kernel(c_ref, w_ref, o_ref):
    dh, kb = w_ref.shape[1], c_ref.shape[0]
    ones_row = (lax.broadcasted_iota(I32, (ONES_ROWS, kb), 0) == 0).astype(o_ref.dtype)
    for h in range(w_ref.shape[0]):
        vt = lax.dot_general(w_ref[h], c_ref[...], (((1,), (1,)), ((), ())), preferred_element_type=F32)
        o_ref[0, h, :dh, :] = vt.astype(o_ref.dtype)
        o_ref[0, h, dh:, :] = ones_row


def _vproj_t(c, wuvt, kb):
    m, cdim = c.shape
    heads, dh, _ = wuvt.shape
    return pl.pallas_call(
        _vproj_t_kernel,
        grid=(m // kb,),
        in_specs=[pl.BlockSpec((kb, cdim), lambda i: (i, 0)), pl.BlockSpec((heads, dh, cdim), lambda i: (0, 0, 0))],
        out_specs=pl.BlockSpec((1, heads, dh + ONES_ROWS, kb), lambda i: (i, 0, 0, 0)),
        out_shape=jax.ShapeDtypeStruct((m // kb, heads, dh + ONES_ROWS, kb), BF16),
        compiler_params=_cparams("parallel"),
        name="vproj_t",
    )(c, wuvt)


def _outproj_kernel(ya_ref, yb_ref, wa_ref, wb_ref, x_ref, o_ref):
    acc = jnp.dot(ya_ref[...], wa_ref[...], preferred_element_type=F32)
    acc = acc + jnp.dot(yb_ref[...], wb_ref[...], preferred_element_type=F32)
    o_ref[...] = x_ref[...] + acc


def _outproj(ya, yb, wa, wb, x):
    m, ka = ya.shape
    kbw = yb.shape[1]
    n = wa.shape[1]
    tm, tn = _tile(m, 1024), _tile(n, 512)
    return pl.pallas_call(
        _outproj_kernel,
        grid=(n // tn, m // tm),
        in_specs=[pl.BlockSpec((tm, ka), lambda j, i: (i, 0)), pl.BlockSpec((tm, kbw), lambda j, i: (i, 0)),
                  pl.BlockSpec((ka, tn), lambda j, i: (0, j)), pl.BlockSpec((kbw, tn), lambda j, i: (0, j)),
                  pl.BlockSpec((tm, tn), lambda j, i: (i, j))],
        out_specs=pl.BlockSpec((tm, tn), lambda j, i: (i, j)),
        out_shape=jax.ShapeDtypeStruct((m, n), F32),
        compiler_params=_cparams("parallel", "parallel"),
        name="outproj",
    )(ya, yb, wa, wb, x)


def kernel(x, norm_g, w_in, kv_norm_g, idx_k_norm_g, idx_k_norm_b, w_uk, w_uv, v_norm_g, v_norm_b, w_s, b_s,
           w_out, final_norm_g):
    batch, seq, d = x.shape
    depth = w_in.shape[0]
    cdim, heads, dh = w_uk.shape[1:]
    aw = heads * dh
    di = idx_k_norm_g.shape[1]
    bw = v_norm_g.shape[1]
    iheads = (w_in.shape[2] - 2 * aw - cdim - di - 3 * bw) // (di + 1)
    kb = _tile(seq, 512)

    names = ("q", "c_kv", "gate_a", "q_idx", "k_idx", "w_idx", "u", "v", "gate_b")
    sizes = (aw, cdim, aw, iheads * di, di, iheads, bw, bw, bw)
    seg, off = {}, 0
    for nm, sz in zip(names, sizes):
        seg[nm] = (off, off + sz)
        off += sz

    xf = x.reshape(batch * seq, d)
    for l in range(depth):
        w = {nm: w_in[l][:, a:b].astype(BF16) for nm, (a, b) in seg.items()}
        h = _rmsnorm(xf, norm_g[l], BF16)
        q = _proj(h, w["q"], "none", BF16, "proj_q")
        c = _kv_latent(h, w["c_kv"], kv_norm_g[l])
        kproj = _proj(c, w_uk[l].reshape(cdim, aw).astype(BF16), "none", BF16, "proj_k", scale=(dh ** -0.5) * LOG2E)
        vt = _vproj_t(c, jnp.transpose(w_uv[l], (1, 2, 0)).astype(BF16), kb)
        gate_a = _proj(h, w["gate_a"], "silu", F32, "proj_gate_a")
        qidx = _proj(h, w["q_idx"], "none", BF16, "proj_q_idx")
        kidx, wt = _idx_small(h, w["k_idx"], w["w_idx"].T, idx_k_norm_g[l], idx_k_norm_b[l],
                              (iheads ** -0.5) * (di ** -0.5))
        y_a = _sparse_attention(q, qidx, wt, gate_a, kidx, kproj, vt, batch=batch, kb=kb)
        u = _proj(h, w["u"], "gelu", F32, "proj_u")
        v = _proj(h, w["v"], "gelu", F32, "proj_v")
        gate_b = _proj(h, w["gate_b"], "silu", F32, "proj_gate_b")
        y_b = _spatial(u, v, gate_b, w_s[l], b_s[l], v_norm_g[l], v_norm_b[l])
        wo = w_out[l].astype(BF16)
        xf = _outproj(y_a, y_b, wo[:aw], wo[aw:], xf)
    return _rmsnorm(xf, final_norm_g, x.dtype).reshape(batch, seq, d)
```

```python
import functools
import math

import jax
import jax.numpy as jnp
from jax import lax
from jax.experimental import pallas as pl
from jax.experimental.pallas import tpu as pltpu

F32, BF16, I32 = jnp.float32, jnp.bfloat16, jnp.int32

EPS = 1e-6
TOPK_MAX = 256
SQRT_HALF = 0.7071067811865476
LOG2E = 1.4426950408889634

V7X_LANES = 128
V7X_SUBLANES = 8
V7X_VMEM_LIMIT_BYTES = 56 * 2**20

INT_MIN = -(2**31)
MASK_NEG = -1e30


def _cparams(*sem):
    return pltpu.CompilerParams(dimension_semantics=sem, vmem_limit_bytes=V7X_VMEM_LIMIT_BYTES)


def _tile(n, want):
    t = min(n, want)
    while n % t:
        t //= 2
    return t


def _rmsnorm_kernel(x_ref, g_ref, o_ref):
    x = x_ref[...]
    ms = jnp.mean(x * x, axis=-1, keepdims=True)
    o_ref[...] = (x * lax.rsqrt(ms + EPS) * g_ref[...]).astype(o_ref.dtype)


def _rmsnorm(x, g, out_dtype):
    m, d = x.shape
    tm = _tile(m, 256)
    return pl.pallas_call(
        _rmsnorm_kernel,
        grid=(m // tm,),
        in_specs=[pl.BlockSpec((tm, d), lambda i: (i, 0)), pl.BlockSpec((1, d), lambda i: (0, 0))],
        out_specs=pl.BlockSpec((tm, d), lambda i: (i, 0)),
        out_shape=jax.ShapeDtypeStruct((m, d), out_dtype),
        compiler_params=_cparams("parallel"),
        name="rmsnorm",
    )(x, g.reshape(1, d))


def _gelu(x):
    return 0.5 * x * (1.0 + lax.erf(x * SQRT_HALF))


_ACTS = {"none": lambda x: x, "silu": jax.nn.silu, "gelu": _gelu}


def _proj_kernel(h_ref, w_ref, o_ref, *, act, scale):
    acc = jnp.dot(h_ref[...], w_ref[...], preferred_element_type=F32)
    if scale is not None:
        acc = acc * scale
    o_ref[...] = _ACTS[act](acc).astype(o_ref.dtype)


def _proj(h, w, act, out_dtype, name, scale=None, cols=None):
    m, k = h.shape
    c0, n = cols if cols is not None else (0, w.shape[1])
    tm, tn = _tile(m, 1024), _tile(math.gcd(n, c0), 512)
    j0 = c0 // tn
    return pl.pallas_call(
        functools.partial(_proj_kernel, act=act, scale=scale),
        grid=(n // tn, m // tm),
        in_specs=[pl.BlockSpec((tm, k), lambda j, i: (i, 0)), pl.BlockSpec((k, tn), lambda j, i: (0, j0 + j))],
        out_specs=pl.BlockSpec((tm, tn), lambda j, i: (i, j)),
        out_shape=jax.ShapeDtypeStruct((m, n), out_dtype),
        compiler_params=_cparams("parallel", "parallel"),
        name=name,
    )(h, w)


def _kv_kernel(h_ref, w_ref, g_ref, c_ref):
    c = jnp.dot(h_ref[...], w_ref[...], preferred_element_type=F32)
    ms = jnp.mean(c * c, axis=-1, keepdims=True)
    c_ref[...] = (c * lax.rsqrt(ms + EPS) * g_ref[...]).astype(c_ref.dtype)


def _kv_latent(h, w, cols, g):
    m, k = h.shape
    c0, c = cols
    assert c0 % c == 0
    tm = _tile(m, 512)
    return pl.pallas_call(
        _kv_kernel,
        grid=(m // tm,),
        in_specs=[pl.BlockSpec((tm, k), lambda i: (i, 0)), pl.BlockSpec((k, c), lambda i: (0, c0 // c)),
                  pl.BlockSpec((1, c), lambda i: (0, 0))],
        out_specs=pl.BlockSpec((tm, c), lambda i: (i, 0)),
        out_shape=jax.ShapeDtypeStruct((m, c), BF16),
        compiler_params=_cparams("parallel"),
        name="kv_latent",
    )(h, w, g.reshape(1, c))


def _idx_kernel(h_ref, wk_ref, wwt_ref, g_ref, b_ref, k_ref, wt_ref, *, wscale):
    h = h_ref[...]
    k = jnp.dot(h, wk_ref[...], preferred_element_type=F32)
    mu = jnp.mean(k, axis=-1, keepdims=True)
    kc = k - mu
    var = jnp.mean(kc * kc, axis=-1, keepdims=True)
    k_ref[...] = (kc * lax.rsqrt(var + EPS) * g_ref[...] + b_ref[...]).astype(k_ref.dtype)
    wt = lax.dot_general(wwt_ref[...], h, (((1,), (1,)), ((), ())), preferred_element_type=F32)
    wt_ref[...] = wt * wscale


def _idx_small(h, w, kcols, wwt, g, b, wscale):
    m, k = h.shape
    k0, di = kcols
    hi = wwt.shape[0]
    assert k0 % di == 0
    tm = _tile(m, 512)
    return pl.pallas_call(
        functools.partial(_idx_kernel, wscale=wscale),
        grid=(m // tm,),
        in_specs=[pl.BlockSpec((tm, k), lambda i: (i, 0)), pl.BlockSpec((k, di), lambda i: (0, k0 // di)),
                  pl.BlockSpec((hi, k), lambda i: (0, 0)), pl.BlockSpec((1, di), lambda i: (0, 0)),
                  pl.BlockSpec((1, di), lambda i: (0, 0))],
        out_specs=[pl.BlockSpec((tm, di), lambda i: (i, 0)), pl.BlockSpec((hi, tm), lambda i: (0, i))],
        out_shape=[jax.ShapeDtypeStruct((m, di), BF16), jax.ShapeDtypeStruct((hi, m), F32)],
        compiler_params=_cparams("parallel"),
        name="idx_small",
    )(h, w, wwt, g.reshape(1, di), b.reshape(1, di))


def _spatial_kernel(u_ref, v_ref, gb_ref, ws_ref, bst_ref, vg_ref, vb_ref, o_ref, *, chunk, groups):
    tm, bw = v_ref.shape
    e = bw // groups
    v = v_ref[...]
    mu = jnp.mean(v, axis=-1, keepdims=True)
    vc = v - mu
    var = jnp.mean(vc * vc, axis=-1, keepdims=True)
    vn = (vc * lax.rsqrt(var + EPS) * vg_ref[...] + vb_ref[...]).astype(BF16)
    row = lax.broadcasted_iota(I32, (chunk, chunk), 0)
    col = lax.broadcasted_iota(I32, (chunk, chunk), 1)
    causal = col <= row
    for g in range(groups):
        wsg = jnp.where(causal, ws_ref[g], 0.0).astype(BF16)
        bias = jnp.broadcast_to(bst_ref[:, g:g + 1], (chunk, e))
        for c in range(tm // chunk):
            rows, cols = slice(c * chunk, (c + 1) * chunk), slice(g * e, (g + 1) * e)
            mixed = jnp.dot(wsg, vn[rows, cols], preferred_element_type=F32) + bias
            o_ref[rows, cols] = (u_ref[rows, cols] * mixed * gb_ref[rows, cols]).astype(o_ref.dtype)


def _spatial(u, v, gate, w_s, b_s, vg, vb):
    m, bw = u.shape
    groups, chunk, _ = w_s.shape
    tm = 2 * chunk if m % (2 * chunk) == 0 else chunk
    row = lambda i: (i, 0)
    return pl.pallas_call(
        functools.partial(_spatial_kernel, chunk=chunk, groups=groups),
        grid=(m // tm,),
        in_specs=[pl.BlockSpec((tm, bw), row), pl.BlockSpec((tm, bw), row), pl.BlockSpec((tm, bw), row),
                  pl.BlockSpec((groups, chunk, chunk), lambda i: (0, 0, 0)),
                  pl.BlockSpec((chunk, groups), lambda i: (0, 0)),
                  pl.BlockSpec((1, bw), lambda i: (0, 0)), pl.BlockSpec((1, bw), lambda i: (0, 0))],
        out_specs=pl.BlockSpec((tm, bw), row),
        out_shape=jax.ShapeDtypeStruct((m, bw), BF16),
        compiler_params=_cparams("parallel"),
        name="spatial_gating",
    )(u, v, gate, w_s, b_s.T, vg.reshape(1, bw), vb.reshape(1, bw))


IDX_HEAD_GROUP = 4
WORD_BITS = 32
ONES_ROWS = 16
CHUNK_ROWS = WORD_BITS * V7X_SUBLANES


def _bit_transpose32(a):
    a = list(a)
    m, j = 0x0000FFFF, 16
    while j:
        k = 0
        while k < WORD_BITS:
            t = (a[k] ^ lax.shift_right_logical(a[k + j], jnp.int32(j))) & jnp.int32(m)
            a[k] = a[k] ^ t
            a[k + j] = a[k + j] ^ lax.shift_left(t, jnp.int32(j))
            k = (k + j + 1) & ~j
        j >>= 1
        m = (m ^ (m << j)) & 0xFFFFFFFF
    return a


def _attn_kernel(qi_ref, ki_ref, q_ref, qidx_ref, wt_ref, gate_ref, kidx_ref, kp_ref, vt_ref, o_ref,
                 planes_ref, alive_ref, sel_ref, qs_ref, bias_ref, acc_ref, m_ref,
                 *, qb, kb, topk, seq_bits):
    t = pl.program_id(1)
    i, kbi = qi_ref[t], ki_ref[t]
    heads, dh = vt_ref.shape[1], vt_ref.shape[2] - ONES_ROWS
    iheads, di = wt_ref.shape[0], kidx_ref.shape[1]
    cpk = kb // CHUNK_ROWS
    n_kb = ((i + 1) * qb + kb - 1) // kb
    nt = (((1,), (1,)), ((), ()))

    @pl.when(kbi == 0)
    def _select():
        for h in range(iheads):
            qs_ref[h * qb:(h + 1) * qb, :] = qidx_ref[:, h * di:(h + 1) * di]
        wt = wt_ref[...]
        qpos = i * qb + lax.broadcasted_iota(I32, (1, qb), 1)

        @pl.when(i == 0)
        def _():
            planes_ref[...] = jnp.zeros(planes_ref.shape, I32)

        def score_block(kbj, masked):
            r0 = pl.multiple_of(kbj * kb, kb)
            kblk = kidx_ref[pl.ds(r0, kb), :]
            score = jnp.zeros((kb, qb), F32)
            for hg in range(iheads // IDX_HEAD_GROUP):
                h0 = hg * IDX_HEAD_GROUP
                lt = lax.dot_general(kblk, qs_ref[h0 * qb:(h0 + IDX_HEAD_GROUP) * qb, :], nt,
                                     preferred_element_type=F32)
                for hh in range(IDX_HEAD_GROUP):
                    score = score + jnp.maximum(lt[:, hh * qb:(hh + 1) * qb], 0.0) * wt[h0 + hh:h0 + hh + 1, :]
            bits = pltpu.bitcast(score, I32)
            key = bits ^ ((bits >> 31) | INT_MIN)
            key = jnp.where(key == 0x7FFFFFFF, INT_MIN, key)
            if masked:
                kpos = r0 + lax.broadcasted_iota(I32, (kb, 1), 0)
                key = jnp.where(kpos <= qpos, key, 0)
            for cc in range(cpk):
                rows = [key[cc * CHUNK_ROWS + j * V7X_SUBLANES:cc * CHUNK_ROWS + (j + 1) * V7X_SUBLANES, :]
                        for j in range(WORD_BITS)]
                planes = _bit_transpose32(rows)
                for b in range(WORD_BITS):
                    planes_ref[b, kbj * cpk + cc] = planes[b]

        def score_body(kbj, carry):
            score_block(kbj, False)
            return carry

        lax.fori_loop(0, n_kb - 1, score_body, 0)
        score_block(n_kb - 1, True)

        def popsum(words):
            return jnp.sum(jnp.sum(lax.population_count(words), axis=0), axis=0, keepdims=True)

        cidx = lax.broadcasted_iota(I32, alive_ref.shape, 0)
        alive_ref[...] = jnp.where(cidx < n_kb * cpk, -1, 0)
        sel_ref[...] = jnp.zeros(sel_ref.shape, I32)

        def radix_body(b, k_rem):
            plane = planes_ref[b]
            alive = alive_ref[...]
            ones = alive & plane
            cnt = popsum(ones)
            take1 = cnt >= k_rem
            alive_ref[...] = jnp.where(take1, ones, alive & ~plane)
            sel_ref[...] = jnp.where(take1, sel_ref[...], sel_ref[...] | ones)
            return jnp.where(take1, k_rem, k_rem - cnt)

        need = lax.fori_loop(0, WORD_BITS, radix_body, jnp.minimum(qpos + 1, topk))

        @pl.when(jnp.max(popsum(alive_ref[...]) - need) > 0)
        def _():
            ties = alive_ref[...]
            row0 = cidx * CHUNK_ROWS + lax.broadcasted_iota(I32, alive_ref.shape, 1)

            def rows_below(lim):
                nj = jnp.clip(lax.shift_right_arithmetic(lim - row0 + (V7X_SUBLANES - 1), 3), 0, WORD_BITS)
                return jnp.where(nj <= 0, 0, lax.shift_left(jnp.int32(-1), WORD_BITS - jnp.maximum(nj, 1)))

            def idx_body(b, lim):
                cand = lim + lax.shift_left(jnp.int32(1), seq_bits - 1 - b)
                return jnp.where(popsum(ties & rows_below(cand)) < need, cand, lim)

            lim = lax.fori_loop(0, seq_bits, idx_body, jnp.zeros((1, qb), I32))
            alive_ref[...] = ties & rows_below(lim + 1)

        sel_ref[...] = sel_ref[...] | alive_ref[...]

        def bias_body(c, carry):
            words = sel_ref[c]
            for j in range(WORD_BITS):
                top = words if j == 0 else lax.shift_left(words, jnp.int32(j))
                r = pl.multiple_of(c * CHUNK_ROWS, CHUNK_ROWS) + j * V7X_SUBLANES
                bias_ref[pl.ds(r, V7X_SUBLANES), :] = jnp.where(top < 0, 0.0, MASK_NEG)
            return carry

        lax.fori_loop(0, n_kb * cpk, bias_body, 0)
        m_ref[...] = jnp.full(m_ref.shape, MASK_NEG, F32)
        acc_ref[...] = jnp.zeros(acc_ref.shape, F32)

    r0 = pl.multiple_of(kbi * kb, kb)
    bias = bias_ref[pl.ds(r0, kb), :]
    s = jnp.concatenate(
        [lax.dot_general(kp_ref[:, h * dh:(h + 1) * dh], q_ref[:, h * dh:(h + 1) * dh], nt,
                         preferred_element_type=F32) + bias for h in range(heads)], axis=1)
    m_old = m_ref[...]
    m_new = jnp.maximum(m_old, jnp.max(s, axis=0, keepdims=True))
    alpha = jnp.exp2(m_old - m_new)
    p = jnp.exp2(s - m_new).astype(vt_ref.dtype)
    m_ref[...] = m_new
    for h in range(heads):
        qc = slice(h * qb, (h + 1) * qb)
        pv = jnp.dot(vt_ref[0, h], p[:, qc], preferred_element_type=F32)
        acc_ref[h] = acc_ref[h] * alpha[:, qc] + pv

    @pl.when(kbi == n_kb - 1)
    def _finish():
        for h in range(heads):
            hc = slice(h * dh, (h + 1) * dh)
            o_t = acc_ref[h, :dh, :] * (1.0 / acc_ref[h, dh:dh + 1, :])
            o_ref[:, hc] = (o_t.T * gate_ref[:, hc]).astype(o_ref.dtype)


def _causal_steps(nq, qb, kb):
    qi, ki = [], []
    for i in range(nq):
        for k in range(((i + 1) * qb + kb - 1) // kb):
            qi.append(i)
            ki.append(k)
    return jnp.asarray(qi, I32), jnp.asarray(ki, I32)


def _sparse_attention(q, qidx, wt, gate, kidx, kproj, vt, *, batch, kb):
    m, aw = q.shape
    seq = m // batch
    heads, dh = vt.shape[1], vt.shape[2] - ONES_ROWS
    iheads, di = wt.shape[0], kidx.shape[1]
    qb = min(2 * V7X_LANES, seq)
    nq, nkb = seq // qb, seq // kb
    assert seq % kb == 0 and kb % CHUNK_ROWS == 0 and iheads % IDX_HEAD_GROUP == 0
    topk = min(TOPK_MAX, seq // 4)
    qi, ki = _causal_steps(nq, qb, kb)
    qrow = lambda b, t, qi, ki: (b * nq + qi[t], 0)
    nchunks = seq // CHUNK_ROWS
    grid_spec = pltpu.PrefetchScalarGridSpec(
        num_scalar_prefetch=2,
        grid=(batch, qi.shape[0]),
        in_specs=[pl.BlockSpec((qb, aw), qrow),
                  pl.BlockSpec((qb, iheads * di), qrow),
                  pl.BlockSpec((iheads, qb), lambda b, t, qi, ki: (0, b * nq + qi[t])),
                  pl.BlockSpec((qb, aw), qrow),
                  pl.BlockSpec((seq, di), lambda b, t, qi, ki: (b, 0), pipeline_mode=pl.Buffered(1)),
                  pl.BlockSpec((kb, aw), lambda b, t, qi, ki: (b * nkb + ki[t], 0)),
                  pl.BlockSpec((1, heads, dh + ONES_ROWS, kb), lambda b, t, qi, ki: (b * nkb + ki[t], 0, 0, 0))],
        out_specs=pl.BlockSpec((qb, aw), qrow),
        scratch_shapes=[pltpu.VMEM((WORD_BITS, nchunks, V7X_SUBLANES, qb), I32),
                        pltpu.VMEM((nchunks, V7X_SUBLANES, qb), I32),
                        pltpu.VMEM((nchunks, V7X_SUBLANES, qb), I32),
                        pltpu.VMEM((iheads * qb, di), BF16),
                        pltpu.VMEM((seq, qb), F32),
                        pltpu.VMEM((heads, dh + ONES_ROWS, qb), F32),
                        pltpu.VMEM((1, heads * qb), F32)])
    return pl.pallas_call(
        functools.partial(_attn_kernel, qb=qb, kb=kb, topk=topk, seq_bits=seq.bit_length()),
        grid_spec=grid_spec,
        out_shape=jax.ShapeDtypeStruct((m, aw), BF16),
        compiler_params=_cparams("parallel", "arbitrary"),
        name="sparse_attention",
    )(qi, ki, q, qidx, wt, gate, kidx, kproj, vt)


def _vproj_t_kernel(c_ref, w_ref, o_ref):
    dh, kb = w_ref.shape[1], c_ref.shape[0]
    ones_row = (lax.broadcasted_iota(I32, (ONES_ROWS, kb), 0) == 0).astype(o_ref.dtype)
    for h in range(w_ref.shape[0]):
        vt = lax.dot_general(w_ref[h], c_ref[...], (((1,), (1,)), ((), ())), preferred_element_type=F32)
        o_ref[0, h, :dh, :] = vt.astype(o_ref.dtype)
        o_ref[0, h, dh:, :] = ones_row


def _vproj_t(c, wuvt, kb):
    m, cdim = c.shape
    heads, dh, _ = wuvt.shape
    return pl.pallas_call(
        _vproj_t_kernel,
        grid=(m // kb,),
        in_specs=[pl.BlockSpec((kb, cdim), lambda i: (i, 0)), pl.BlockSpec((heads, dh, cdim), lambda i: (0, 0, 0))],
        out_specs=pl.BlockSpec((1, heads, dh + ONES_ROWS, kb), lambda i: (i, 0, 0, 0)),
        out_shape=jax.ShapeDtypeStruct((m // kb, heads, dh + ONES_ROWS, kb), BF16),
        compiler_params=_cparams("parallel"),
        name="vproj_t",
    )(c, wuvt)


def _outproj_kernel(ya_ref, yb_ref, wa_ref, wb_ref, x_ref, o_ref):
    acc = jnp.dot(ya_ref[...], wa_ref[...], preferred_element_type=F32)
    acc = acc + jnp.dot(yb_ref[...], wb_ref[...], preferred_element_type=F32)
    o_ref[...] = x_ref[...] + acc


def _outproj(ya, yb, w, x):
    m, ka = ya.shape
    kbw = yb.shape[1]
    n = w.shape[1]
    assert ka == kbw and w.shape[0] == ka + kbw
    tm, tn = _tile(m, 1024), _tile(n, 512)
    return pl.pallas_call(
        _outproj_kernel,
        grid=(n // tn, m // tm),
        in_specs=[pl.BlockSpec((tm, ka), lambda j, i: (i, 0)), pl.BlockSpec((tm, kbw), lambda j, i: (i, 0)),
                  pl.BlockSpec((ka, tn), lambda j, i: (0, j)), pl.BlockSpec((kbw, tn), lambda j, i: (1, j)),
                  pl.BlockSpec((tm, tn), lambda j, i: (i, j))],
        out_specs=pl.BlockSpec((tm, tn), lambda j, i: (i, j)),
        out_shape=jax.ShapeDtypeStruct((m, n), F32),
        compiler_params=_cparams("parallel", "parallel"),
        name="outproj",
    )(ya, yb, w, w, x)


def kernel(x, norm_g, w_in, kv_norm_g, idx_k_norm_g, idx_k_norm_b, w_uk, w_uv, v_norm_g, v_norm_b, w_s, b_s,
           w_out, final_norm_g):
    batch, seq, d = x.shape
    depth = w_in.shape[0]
    cdim, heads, dh = w_uk.shape[1:]
    aw = heads * dh
    di = idx_k_norm_g.shape[1]
    bw = v_norm_g.shape[1]
    iheads = (w_in.shape[2] - 2 * aw - cdim - di - 3 * bw) // (di + 1)
    kb = _tile(seq, 512)

    names = ("q", "c_kv", "gate_a", "q_idx", "k_idx", "w_idx", "u", "v", "gate_b")
    sizes = (aw, cdim, aw, iheads * di, di, iheads, bw, bw, bw)
    seg, off = {}, 0
    for nm, sz in zip(names, sizes):
        seg[nm] = (off, off + sz)
        off += sz

    w_in_bf, w_out_bf = w_in.astype(BF16), w_out.astype(BF16)
    b0 = seg["u"][0]
    cols = lambda nm: (seg[nm][0], seg[nm][1] - seg[nm][0])
    bcols = lambda nm: (seg[nm][0] - b0, seg[nm][1] - seg[nm][0])

    xf = x.reshape(batch * seq, d)
    for l in range(depth):
        wl = w_in_bf[l]
        wb = wl[:, b0:]
        h = _rmsnorm(xf, norm_g[l], BF16)
        q = _proj(h, wl, "none", BF16, "proj_q", cols=cols("q"))
        c = _kv_latent(h, wl, cols("c_kv"), kv_norm_g[l])
        kproj = _proj(c, w_uk[l].reshape(cdim, aw).astype(BF16), "none", BF16, "proj_k", scale=(dh ** -0.5) * LOG2E)
        vt = _vproj_t(c, jnp.transpose(w_uv[l], (1, 2, 0)).astype(BF16), kb)
        gate_a = _proj(h, wl, "silu", F32, "proj_gate_a", cols=cols("gate_a"))
        qidx = _proj(h, wl, "none", BF16, "proj_q_idx", cols=cols("q_idx"))
        wwt = wl[:, seg["w_idx"][0]:seg["w_idx"][1]].T
        kidx, wt = _idx_small(h, wl, cols("k_idx"), wwt, idx_k_norm_g[l], idx_k_norm_b[l],
                              (iheads ** -0.5) * (di ** -0.5))
        y_a = _sparse_attention(q, qidx, wt, gate_a, kidx, kproj, vt, batch=batch, kb=kb)
        u = _proj(h, wb, "gelu", F32, "proj_u", cols=bcols("u"))
        v = _proj(h, wb, "gelu", F32, "proj_v", cols=bcols("v"))
        gate_b = _proj(h, wb, "silu", F32, "proj_gate_b", cols=bcols("gate_b"))
        y_b = _spatial(u, v, gate_b, w_s[l], b_s[l], v_norm_g[l], v_norm_b[l])
        xf = _outproj(y_a, y_b, w_out_bf[l], xf)
    return _rmsnorm(xf, final_norm_g, x.dtype).reshape(batch, seq, d)
```

```python
import functools
import math

import jax
import jax.numpy as jnp
from jax import lax
from jax.experimental import pallas as pl
from jax.experimental.pallas import tpu as pltpu

F32, BF16, I32 = jnp.float32, jnp.bfloat16, jnp.int32

EPS = 1e-6
TOPK_MAX = 256
SQRT_HALF = 0.7071067811865476
LOG2E = 1.4426950408889634

V7X_LANES = 128
V7X_SUBLANES = 8
V7X_VMEM_LIMIT_BYTES = 56 * 2**20

ONES_ROWS = 16

INT_MIN = -(2**31)
MASK_NEG = -1e30


def _cparams(*sem):
    return pltpu.CompilerParams(dimension_semantics=sem, vmem_limit_bytes=V7X_VMEM_LIMIT_BYTES)


def _tile(n, want):
    t = min(n, want)
    while n % t:
        t //= 2
    return t


def _rmsnorm_kernel(x_ref, g_ref, o_ref):
    x = x_ref[...]
    ms = jnp.mean(x * x, axis=-1, keepdims=True)
    o_ref[...] = (x * lax.rsqrt(ms + EPS) * g_ref[...]).astype(o_ref.dtype)


def _rmsnorm(x, g, out_dtype):
    m, d = x.shape
    tm = _tile(m, 256)
    return pl.pallas_call(
        _rmsnorm_kernel,
        grid=(m // tm,),
        in_specs=[pl.BlockSpec((tm, d), lambda i: (i, 0)), pl.BlockSpec((1, d), lambda i: (0, 0))],
        out_specs=pl.BlockSpec((tm, d), lambda i: (i, 0)),
        out_shape=jax.ShapeDtypeStruct((m, d), out_dtype),
        compiler_params=_cparams("parallel"),
        name="rmsnorm",
    )(x, g.reshape(1, d))


def _gelu(x):
    return 0.5 * x * (1.0 + lax.erf(x * SQRT_HALF))


_ACTS = {"none": lambda x: x, "silu": jax.nn.silu, "gelu": _gelu}


def _proj_kernel(h_ref, w_ref, o_ref, *, act):
    acc = jnp.dot(h_ref[...], w_ref[...], preferred_element_type=F32)
    o_ref[...] = _ACTS[act](acc).astype(o_ref.dtype)


def _wspec(w, layer, rows, cols, index):
    return pl.BlockSpec((None, rows, cols), lambda *ids: (layer,) + tuple(index(*ids)))


def _proj(h, w, layer, cols, act, out_dtype, name):
    m, k = h.shape
    c0, n = cols
    tm, tn = _tile(m, 1024), _tile(math.gcd(n, c0), 512)
    j0 = c0 // tn
    return pl.pallas_call(
        functools.partial(_proj_kernel, act=act),
        grid=(n // tn, m // tm),
        in_specs=[pl.BlockSpec((tm, k), lambda j, i: (i, 0)), _wspec(w, layer, k, tn, lambda j, i: (0, j0 + j))],
        out_specs=pl.BlockSpec((tm, tn), lambda j, i: (i, j)),
        out_shape=jax.ShapeDtypeStruct((m, n), out_dtype),
        compiler_params=_cparams("parallel", "parallel"),
        name=name,
    )(h, w)


def _latent_kernel(h_ref, w_ref, g_ref, wuk_ref, wuvt_ref, k_ref, vt_ref, *, kscale):
    c = jnp.dot(h_ref[...], w_ref[...], preferred_element_type=F32)
    ms = jnp.mean(c * c, axis=-1, keepdims=True)
    c = (c * lax.rsqrt(ms + EPS) * g_ref[...]).astype(BF16)
    k_ref[...] = (jnp.dot(c, wuk_ref[...], preferred_element_type=F32) * kscale).astype(k_ref.dtype)
    heads, dh, kb = wuvt_ref.shape[0], wuvt_ref.shape[1], c.shape[0]
    ones_row = (lax.broadcasted_iota(I32, (ONES_ROWS, kb), 0) == 0).astype(vt_ref.dtype)
    for hd in range(heads):
        vt = lax.dot_general(wuvt_ref[hd], c, (((1,), (1,)), ((), ())), preferred_element_type=F32)
        vt_ref[0, hd, :dh, :] = vt.astype(vt_ref.dtype)
        vt_ref[0, hd, dh:, :] = ones_row


def _latent_keys_values(h, w, layer, cols, g, wuk, wuvt, kscale, kb):
    m, k = h.shape
    c0, cdim = cols
    heads, dh, _ = wuvt.shape
    assert c0 % cdim == 0 and m % kb == 0
    return pl.pallas_call(
        functools.partial(_latent_kernel, kscale=kscale),
        grid=(m // kb,),
        in_specs=[pl.BlockSpec((kb, k), lambda i: (i, 0)), _wspec(w, layer, k, cdim, lambda i: (0, c0 // cdim)),
                  pl.BlockSpec((1, cdim), lambda i: (0, 0)), pl.BlockSpec((cdim, heads * dh), lambda i: (0, 0)),
                  pl.BlockSpec((heads, dh, cdim), lambda i: (0, 0, 0))],
        out_specs=[pl.BlockSpec((kb, heads * dh), lambda i: (i, 0)),
                   pl.BlockSpec((1, heads, dh + ONES_ROWS, kb), lambda i: (i, 0, 0, 0))],
        out_shape=[jax.ShapeDtypeStruct((m, heads * dh), BF16),
                   jax.ShapeDtypeStruct((m // kb, heads, dh + ONES_ROWS, kb), BF16)],
        compiler_params=_cparams("parallel"),
        name="latent_keys_values",
    )(h, w, g.reshape(1, cdim), wuk, wuvt)


def _idx_kernel(h_ref, wk_ref, wwt_ref, g_ref, b_ref, k_ref, wt_ref, *, wscale):
    h = h_ref[...]
    k = jnp.dot(h, wk_ref[...], preferred_element_type=F32)
    mu = jnp.mean(k, axis=-1, keepdims=True)
    kc = k - mu
    var = jnp.mean(kc * kc, axis=-1, keepdims=True)
    k_ref[...] = (kc * lax.rsqrt(var + EPS) * g_ref[...] + b_ref[...]).astype(k_ref.dtype)
    wt = lax.dot_general(wwt_ref[...], h, (((1,), (1,)), ((), ())), preferred_element_type=F32)
    wt_ref[...] = wt * wscale


def _idx_small(h, w, layer, kcols, wwt, g, b, wscale):
    m, k = h.shape
    k0, di = kcols
    hi = wwt.shape[0]
    assert k0 % di == 0
    tm = _tile(m, 512)
    return pl.pallas_call(
        functools.partial(_idx_kernel, wscale=wscale),
        grid=(m // tm,),
        in_specs=[pl.BlockSpec((tm, k), lambda i: (i, 0)), _wspec(w, layer, k, di, lambda i: (0, k0 // di)),
                  pl.BlockSpec((hi, k), lambda i: (0, 0)), pl.BlockSpec((1, di), lambda i: (0, 0)),
                  pl.BlockSpec((1, di), lambda i: (0, 0))],
        out_specs=[pl.BlockSpec((tm, di), lambda i: (i, 0)), pl.BlockSpec((hi, tm), lambda i: (0, i))],
        out_shape=[jax.ShapeDtypeStruct((m, di), BF16), jax.ShapeDtypeStruct((hi, m), F32)],
        compiler_params=_cparams("parallel"),
        name="idx_small",
    )(h, w, wwt, g.reshape(1, di), b.reshape(1, di))


def _spatial_kernel(u_ref, v_ref, gb_ref, ws_ref, bst_ref, vg_ref, vb_ref, o_ref, *, chunk, groups):
    tm, bw = v_ref.shape
    e = bw // groups
    v = v_ref[...]
    mu = jnp.mean(v, axis=-1, keepdims=True)
    vc = v - mu
    var = jnp.mean(vc * vc, axis=-1, keepdims=True)
    vn = (vc * lax.rsqrt(var + EPS) * vg_ref[...] + vb_ref[...]).astype(BF16)
    row = lax.broadcasted_iota(I32, (chunk, chunk), 0)
    col = lax.broadcasted_iota(I32, (chunk, chunk), 1)
    causal = col <= row
    for g in range(groups):
        wsg = jnp.where(causal, ws_ref[g], 0.0).astype(BF16)
        bias = jnp.broadcast_to(bst_ref[:, g:g + 1], (chunk, e))
        for c in range(tm // chunk):
            rows, cols = slice(c * chunk, (c + 1) * chunk), slice(g * e, (g + 1) * e)
            mixed = jnp.dot(wsg, vn[rows, cols], preferred_element_type=F32) + bias
            o_ref[rows, cols] = (u_ref[rows, cols] * mixed * gb_ref[rows, cols]).astype(o_ref.dtype)


def _spatial(u, v, gate, w_s, b_s, vg, vb):
    m, bw = u.shape
    groups, chunk, _ = w_s.shape
    tm = 2 * chunk if m % (2 * chunk) == 0 else chunk
    row = lambda i: (i, 0)
    return pl.pallas_call(
        functools.partial(_spatial_kernel, chunk=chunk, groups=groups),
        grid=(m // tm,),
        in_specs=[pl.BlockSpec((tm, bw), row), pl.BlockSpec((tm, bw), row), pl.BlockSpec((tm, bw), row),
                  pl.BlockSpec((groups, chunk, chunk), lambda i: (0, 0, 0)),
                  pl.BlockSpec((chunk, groups), lambda i: (0, 0)),
                  pl.BlockSpec((1, bw), lambda i: (0, 0)), pl.BlockSpec((1, bw), lambda i: (0, 0))],
        out_specs=pl.BlockSpec((tm, bw), row),
        out_shape=jax.ShapeDtypeStruct((m, bw), BF16),
        compiler_params=_cparams("parallel"),
        name="spatial_gating",
    )(u, v, gate, w_s, b_s.T, vg.reshape(1, bw), vb.reshape(1, bw))


IDX_HEAD_GROUP = 4
WORD_BITS = 32
CHUNK_ROWS = WORD_BITS * V7X_SUBLANES


def _bit_transpose32(a):
    a = list(a)
    m, j = 0x0000FFFF, 16
    while j:
        k = 0
        while k < WORD_BITS:
            t = (a[k] ^ lax.shift_right_logical(a[k + j], jnp.int32(j))) & jnp.int32(m)
            a[k] = a[k] ^ t
            a[k + j] = a[k + j] ^ lax.shift_left(t, jnp.int32(j))
            k = (k + j + 1) & ~j
        j >>= 1
        m = (m ^ (m << j)) & 0xFFFFFFFF
    return a


def _attn_kernel(qi_ref, ki_ref, q_ref, qidx_ref, wt_ref, gate_ref, kidx_ref, kp_ref, vt_ref, o_ref,
                 planes_ref, alive_ref, sel_ref, qs_ref, bias_ref, acc_ref, m_ref,
                 *, qb, kb, topk, seq_bits):
    t = pl.program_id(1)
    i, kbi = qi_ref[t], ki_ref[t]
    heads, dh = vt_ref.shape[1], vt_ref.shape[2] - ONES_ROWS
    iheads, di = wt_ref.shape[0], kidx_ref.shape[1]
    cpk = kb // CHUNK_ROWS
    n_kb = ((i + 1) * qb + kb - 1) // kb
    nt = (((1,), (1,)), ((), ()))

    @pl.when(kbi == 0)
    def _select():
        for h in range(iheads):
            qs_ref[h * qb:(h + 1) * qb, :] = qidx_ref[:, h * di:(h + 1) * di]
        wt = wt_ref[...]
        qpos = i * qb + lax.broadcasted_iota(I32, (1, qb), 1)

        @pl.when(i == 0)
        def _():
            planes_ref[...] = jnp.zeros(planes_ref.shape, I32)

        def score_block(kbj, masked):
            r0 = pl.multiple_of(kbj * kb, kb)
            kblk = kidx_ref[pl.ds(r0, kb), :]
            score = jnp.zeros((kb, qb), F32)
            for hg in range(iheads // IDX_HEAD_GROUP):
                h0 = hg * IDX_HEAD_GROUP
                lt = lax.dot_general(kblk, qs_ref[h0 * qb:(h0 + IDX_HEAD_GROUP) * qb, :], nt,
                                     preferred_element_type=F32)
                for hh in range(IDX_HEAD_GROUP):
                    score = score + jnp.maximum(lt[:, hh * qb:(hh + 1) * qb], 0.0) * wt[h0 + hh:h0 + hh + 1, :]
            bits = pltpu.bitcast(score, I32)
            key = bits ^ ((bits >> 31) | INT_MIN)
            key = jnp.where(key == 0x7FFFFFFF, INT_MIN, key)
            if masked:
                kpos = r0 + lax.broadcasted_iota(I32, (kb, 1), 0)
                key = jnp.where(kpos <= qpos, key, 0)
            for cc in range(cpk):
                rows = [key[cc * CHUNK_ROWS + j * V7X_SUBLANES:cc * CHUNK_ROWS + (j + 1) * V7X_SUBLANES, :]
                        for j in range(WORD_BITS)]
                planes = _bit_transpose32(rows)
                for b in range(WORD_BITS):
                    planes_ref[b, kbj * cpk + cc] = planes[b]

        def score_body(kbj, carry):
            score_block(kbj, False)
            return carry

        lax.fori_loop(0, n_kb - 1, score_body, 0)
        score_block(n_kb - 1, True)

        def popsum(words):
            return jnp.sum(jnp.sum(lax.population_count(words), axis=0), axis=0, keepdims=True)

        cidx = lax.broadcasted_iota(I32, alive_ref.shape, 0)
        alive_ref[...] = jnp.where(cidx < n_kb * cpk, -1, 0)
        sel_ref[...] = jnp.zeros(sel_ref.shape, I32)

        def radix_body(b, k_rem):
            plane = planes_ref[b]
            alive = alive_ref[...]
            ones = alive & plane
            cnt = popsum(ones)
            take1 = cnt >= k_rem
            alive_ref[...] = jnp.where(take1, ones, alive & ~plane)
            sel_ref[...] = jnp.where(take1, sel_ref[...], sel_ref[...] | ones)
            return jnp.where(take1, k_rem, k_rem - cnt)

        need = lax.fori_loop(0, WORD_BITS, radix_body, jnp.minimum(qpos + 1, topk))

        @pl.when(jnp.max(popsum(alive_ref[...]) - need) > 0)
        def _():
            ties = alive_ref[...]
            row0 = cidx * CHUNK_ROWS + lax.broadcasted_iota(I32, alive_ref.shape, 1)

            def rows_below(lim):
                nj = jnp.clip(lax.shift_right_arithmetic(lim - row0 + (V7X_SUBLANES - 1), 3), 0, WORD_BITS)
                return jnp.where(nj <= 0, 0, lax.shift_left(jnp.int32(-1), WORD_BITS - jnp.maximum(nj, 1)))

            def idx_body(b, lim):
                cand = lim + lax.shift_left(jnp.int32(1), seq_bits - 1 - b)
                return jnp.where(popsum(ties & rows_below(cand)) < need, cand, lim)

            lim = lax.fori_loop(0, seq_bits, idx_body, jnp.zeros((1, qb), I32))
            alive_ref[...] = ties & rows_below(lim + 1)

        sel_ref[...] = sel_ref[...] | alive_ref[...]

        def bias_body(c, carry):
            words = sel_ref[c]
            for j in range(WORD_BITS):
                top = words if j == 0 else lax.shift_left(words, jnp.int32(j))
                r = pl.multiple_of(c * CHUNK_ROWS, CHUNK_ROWS) + j * V7X_SUBLANES
                bias_ref[pl.ds(r, V7X_SUBLANES), :] = jnp.where(top < 0, 0.0, MASK_NEG)
            return carry

        lax.fori_loop(0, n_kb * cpk, bias_body, 0)
        m_ref[...] = jnp.full(m_ref.shape, MASK_NEG, F32)
        acc_ref[...] = jnp.zeros(acc_ref.shape, F32)

    r0 = pl.multiple_of(kbi * kb, kb)
    bias = bias_ref[pl.ds(r0, kb), :]
    s = jnp.concatenate(
        [lax.dot_general(kp_ref[:, h * dh:(h + 1) * dh], q_ref[:, h * dh:(h + 1) * dh], nt,
                         preferred_element_type=F32) + bias for h in range(heads)], axis=1)
    m_old = m_ref[...]
    m_new = jnp.maximum(m_old, jnp.max(s, axis=0, keepdims=True))
    alpha = jnp.exp2(m_old - m_new)
    p = jnp.exp2(s - m_new).astype(vt_ref.dtype)
    m_ref[...] = m_new
    for h in range(heads):
        qc = slice(h * qb, (h + 1) * qb)
        pv = jnp.dot(vt_ref[0, h], p[:, qc], preferred_element_type=F32)
        acc_ref[h] = acc_ref[h] * alpha[:, qc] + pv

    @pl.when(kbi == n_kb - 1)
    def _finish():
        for h in range(heads):
            hc = slice(h * dh, (h + 1) * dh)
            o_t = acc_ref[h, :dh, :] * (1.0 / acc_ref[h, dh:dh + 1, :])
            o_ref[:, hc] = (o_t.T * gate_ref[:, hc]).astype(o_ref.dtype)


def _causal_steps(nq, qb, kb):
    qi, ki = [], []
    for i in range(nq):
        for k in range(((i + 1) * qb + kb - 1) // kb):
            qi.append(i)
            ki.append(k)
    return jnp.asarray(qi, I32), jnp.asarray(ki, I32)


def _sparse_attention(q, qidx, wt, gate, kidx, kproj, vt, *, batch, kb):
    m, aw = q.shape
    seq = m // batch
    heads, dh = vt.shape[1], vt.shape[2] - ONES_ROWS
    iheads, di = wt.shape[0], kidx.shape[1]
    qb = min(2 * V7X_LANES, seq)
    nq, nkb = seq // qb, seq // kb
    assert seq % kb == 0 and kb % CHUNK_ROWS == 0 and iheads % IDX_HEAD_GROUP == 0
    topk = min(TOPK_MAX, seq // 4)
    qi, ki = _causal_steps(nq, qb, kb)
    qrow = lambda b, t, qi, ki: (b * nq + qi[t], 0)
    nchunks = seq // CHUNK_ROWS
    grid_spec = pltpu.PrefetchScalarGridSpec(
        num_scalar_prefetch=2,
        grid=(batch, qi.shape[0]),
        in_specs=[pl.BlockSpec((qb, aw), qrow),
                  pl.BlockSpec((qb, iheads * di), qrow),
                  pl.BlockSpec((iheads, qb), lambda b, t, qi, ki: (0, b * nq + qi[t])),
                  pl.BlockSpec((qb, aw), qrow),
                  pl.BlockSpec((seq, di), lambda b, t, qi, ki: (b, 0), pipeline_mode=pl.Buffered(1)),
                  pl.BlockSpec((kb, aw), lambda b, t, qi, ki: (b * nkb + ki[t], 0)),
                  pl.BlockSpec((1, heads, dh + ONES_ROWS, kb), lambda b, t, qi, ki: (b * nkb + ki[t], 0, 0, 0))],
        out_specs=pl.BlockSpec((qb, aw), qrow),
        scratch_shapes=[pltpu.VMEM((WORD_BITS, nchunks, V7X_SUBLANES, qb), I32),
                        pltpu.VMEM((nchunks, V7X_SUBLANES, qb), I32),
                        pltpu.VMEM((nchunks, V7X_SUBLANES, qb), I32),
                        pltpu.VMEM((iheads * qb, di), BF16),
                        pltpu.VMEM((seq, qb), F32),
                        pltpu.VMEM((heads, dh + ONES_ROWS, qb), F32),
                        pltpu.VMEM((1, heads * qb), F32)])
    return pl.pallas_call(
        functools.partial(_attn_kernel, qb=qb, kb=kb, topk=topk, seq_bits=seq.bit_length()),
        grid_spec=grid_spec,
        out_shape=jax.ShapeDtypeStruct((m, aw), BF16),
        compiler_params=_cparams("parallel", "arbitrary"),
        name="sparse_attention",
    )(qi, ki, q, qidx, wt, gate, kidx, kproj, vt)


def _outproj_kernel(ya_ref, yb_ref, wa_ref, wb_ref, x_ref, o_ref):
    acc = jnp.dot(ya_ref[...], wa_ref[...], preferred_element_type=F32)
    acc = acc + jnp.dot(yb_ref[...], wb_ref[...], preferred_element_type=F32)
    o_ref[...] = x_ref[...] + acc


def _outproj(ya, yb, w, layer, x):
    m, ka = ya.shape
    kbw = yb.shape[1]
    n = w.shape[2]
    assert ka == kbw and w.shape[1] == ka + kbw
    tm, tn = _tile(m, 1024), _tile(n, 512)
    return pl.pallas_call(
        _outproj_kernel,
        grid=(n // tn, m // tm),
        in_specs=[pl.BlockSpec((tm, ka), lambda j, i: (i, 0)), pl.BlockSpec((tm, kbw), lambda j, i: (i, 0)),
                  _wspec(w, layer, ka, tn, lambda j, i: (0, j)), _wspec(w, layer, kbw, tn, lambda j, i: (1, j)),
                  pl.BlockSpec((tm, tn), lambda j, i: (i, j))],
        out_specs=pl.BlockSpec((tm, tn), lambda j, i: (i, j)),
        out_shape=jax.ShapeDtypeStruct((m, n), F32),
        compiler_params=_cparams("parallel", "parallel"),
        name="outproj",
    )(ya, yb, w, w, x)


def kernel(x, norm_g, w_in, kv_norm_g, idx_k_norm_g, idx_k_norm_b, w_uk, w_uv, v_norm_g, v_norm_b, w_s, b_s,
           w_out, final_norm_g):
    batch, seq, d = x.shape
    depth = w_in.shape[0]
    cdim, heads, dh = w_uk.shape[1:]
    aw = heads * dh
    di = idx_k_norm_g.shape[1]
    bw = v_norm_g.shape[1]
    iheads = (w_in.shape[2] - 2 * aw - cdim - di - 3 * bw) // (di + 1)
    kb = _tile(seq, 512)

    names = ("q", "c_kv", "gate_a", "q_idx", "k_idx", "w_idx", "u", "v", "gate_b")
    sizes = (aw, cdim, aw, iheads * di, di, iheads, bw, bw, bw)
    seg, off = {}, 0
    for nm, sz in zip(names, sizes):
        seg[nm] = (off, off + sz)
        off += sz

    w_in_bf, w_out_bf = w_in.astype(BF16), w_out.astype(BF16)
    b0 = seg["u"][0]
    w_b = w_in_bf[:, :, b0:]
    w_widx_t = jnp.swapaxes(w_in_bf[:, :, seg["w_idx"][0]:seg["w_idx"][1]], 1, 2)
    cols = lambda nm: (seg[nm][0], seg[nm][1] - seg[nm][0])
    bcols = lambda nm: (seg[nm][0] - b0, seg[nm][1] - seg[nm][0])

    xf = x.reshape(batch * seq, d)
    for l in range(depth):
        h = _rmsnorm(xf, norm_g[l], BF16)
        q = _proj(h, w_in_bf, l, cols("q"), "none", BF16, "proj_q")
        kproj, vt = _latent_keys_values(h, w_in_bf, l, cols("c_kv"), kv_norm_g[l],
                                        w_uk[l].reshape(cdim, aw).astype(BF16),
                                        jnp.transpose(w_uv[l], (1, 2, 0)).astype(BF16), (dh ** -0.5) * LOG2E, kb)
        gate_a = _proj(h, w_in_bf, l, cols("gate_a"), "silu", F32, "proj_gate_a")
        qidx = _proj(h, w_in_bf, l, cols("q_idx"), "none", BF16, "proj_q_idx")
        kidx, wt = _idx_small(h, w_in_bf, l, cols("k_idx"), w_widx_t[l], idx_k_norm_g[l], idx_k_norm_b[l],
                              (iheads ** -0.5) * (di ** -0.5))
        y_a = _sparse_attention(q, qidx, wt, gate_a, kidx, kproj, vt, batch=batch, kb=kb)
        u = _proj(h, w_b, l, bcols("u"), "gelu", F32, "proj_u")
        v = _proj(h, w_b, l, bcols("v"), "gelu", F32, "proj_v")
        gate_b = _proj(h, w_b, l, bcols("gate_b"), "silu", F32, "proj_gate_b")
        y_b = _spatial(u, v, gate_b, w_s[l], b_s[l], v_norm_g[l], v_norm_b[l])
        xf = _outproj(y_a, y_b, w_out_bf, l, xf)
    return _rmsnorm(xf, final_norm_g, x.dtype).reshape(batch, seq, d)
```

```python
import functools

import jax
import jax.numpy as jnp
from jax import lax
from jax.experimental import pallas as pl
from jax.experimental.pallas import tpu as pltpu

F32, BF16, I32 = jnp.float32, jnp.bfloat16, jnp.int32

EPS = 1e-6
TOPK_MAX = 256
SQRT_HALF = 0.7071067811865476
LOG2E = 1.4426950408889634

V7X_LANES = 128
V7X_SUBLANES = 8
V7X_VMEM_LIMIT_BYTES = 56 * 2**20

ONES_ROWS = 16

INT_MIN = -(2**31)
MASK_NEG = -1e30


def _cparams(*sem):
    return pltpu.CompilerParams(dimension_semantics=sem, vmem_limit_bytes=V7X_VMEM_LIMIT_BYTES)


def _tile(n, want):
    t = min(n, want)
    while n % t:
        t //= 2
    return t


def _rmsnorm_kernel(x_ref, g_ref, o_ref):
    x = x_ref[...]
    ms = jnp.mean(x * x, axis=-1, keepdims=True)
    o_ref[...] = (x * lax.rsqrt(ms + EPS) * g_ref[...]).astype(o_ref.dtype)


def _rmsnorm(x, g, out_dtype):
    m, d = x.shape
    tm = _tile(m, 256)
    return pl.pallas_call(
        _rmsnorm_kernel,
        grid=(m // tm,),
        in_specs=[pl.BlockSpec((tm, d), lambda i: (i, 0)), pl.BlockSpec((1, d), lambda i: (0, 0))],
        out_specs=pl.BlockSpec((tm, d), lambda i: (i, 0)),
        out_shape=jax.ShapeDtypeStruct((m, d), out_dtype),
        compiler_params=_cparams("parallel"),
        name="rmsnorm",
    )(x, g.reshape(1, d))


def _gelu(x):
    return 0.5 * x * (1.0 + lax.erf(x * SQRT_HALF))


_ACTS = {"none": lambda x: x, "silu": jax.nn.silu, "gelu": _gelu}


def _wwin(layer, row0, rows, col0, cols, step=None):
    a0 = (col0 // V7X_LANES) * V7X_LANES
    shift = col0 - a0
    width = cols + (V7X_LANES if shift else 0)
    assert step is None or step % V7X_LANES == 0

    def index(*ids):
        col = a0 if step is None else pl.multiple_of(a0 + ids[0] * step, V7X_LANES)
        return (layer, row0, col)

    return pl.BlockSpec((pl.Element(1), pl.Element(rows), pl.Element(width)), index), shift


def _proj_kernel(h_ref, w_ref, o_ref, wbf_ref, *, act, shift):
    tn = wbf_ref.shape[1]

    @pl.when(pl.program_id(1) == 0)
    def _():
        wbf_ref[...] = w_ref[0, :, shift:shift + tn].astype(wbf_ref.dtype)

    acc = jnp.dot(h_ref[...], wbf_ref[...], preferred_element_type=F32)
    o_ref[...] = _ACTS[act](acc).astype(o_ref.dtype)


def _proj(h, w, layer, cols, act, out_dtype, name):
    m, k = h.shape
    c0, n = cols
    tm, tn = _tile(m, 1024), _tile(n, 512)
    wspec, shift = _wwin(layer, 0, k, c0, tn, step=tn)
    return pl.pallas_call(
        functools.partial(_proj_kernel, act=act, shift=shift),
        grid=(n // tn, m // tm),
        in_specs=[pl.BlockSpec((tm, k), lambda j, i: (i, 0)), wspec],
        out_specs=pl.BlockSpec((tm, tn), lambda j, i: (i, j)),
        out_shape=jax.ShapeDtypeStruct((m, n), out_dtype),
        scratch_shapes=[pltpu.VMEM((k, tn), BF16)],
        compiler_params=_cparams("parallel", "arbitrary"),
        name=name,
    )(h, w)


def _latent_kernel(h_ref, w_ref, g_ref, wuk_ref, wuvt_ref, k_ref, vt_ref, wbf_ref, *, kscale, shift):
    @pl.when(pl.program_id(0) == 0)
    def _():
        wbf_ref[...] = w_ref[0, :, shift:shift + wbf_ref.shape[1]].astype(wbf_ref.dtype)

    c = jnp.dot(h_ref[...], wbf_ref[...], preferred_element_type=F32)
    ms = jnp.mean(c * c, axis=-1, keepdims=True)
    c = (c * lax.rsqrt(ms + EPS) * g_ref[...]).astype(BF16)
    k_ref[...] = (jnp.dot(c, wuk_ref[...], preferred_element_type=F32) * kscale).astype(k_ref.dtype)
    heads, dh, kb = wuvt_ref.shape[0], wuvt_ref.shape[1], c.shape[0]
    ones_row = (lax.broadcasted_iota(I32, (ONES_ROWS, kb), 0) == 0).astype(vt_ref.dtype)
    for hd in range(heads):
        vt = lax.dot_general(wuvt_ref[hd], c, (((1,), (1,)), ((), ())), preferred_element_type=F32)
        vt_ref[0, hd, :dh, :] = vt.astype(vt_ref.dtype)
        vt_ref[0, hd, dh:, :] = ones_row


def _latent_keys_values(h, w, layer, cols, g, wuk, wuvt, kscale, kb):
    m, k = h.shape
    c0, cdim = cols
    heads, dh, _ = wuvt.shape
    assert m % kb == 0
    wspec, shift = _wwin(layer, 0, k, c0, cdim)
    return pl.pallas_call(
        functools.partial(_latent_kernel, kscale=kscale, shift=shift),
        grid=(m // kb,),
        in_specs=[pl.BlockSpec((kb, k), lambda i: (i, 0)), wspec,
                  pl.BlockSpec((1, cdim), lambda i: (0, 0)), pl.BlockSpec((cdim, heads * dh), lambda i: (0, 0)),
                  pl.BlockSpec((heads, dh, cdim), lambda i: (0, 0, 0))],
        out_specs=[pl.BlockSpec((kb, heads * dh), lambda i: (i, 0)),
                   pl.BlockSpec((1, heads, dh + ONES_ROWS, kb), lambda i: (i, 0, 0, 0))],
        out_shape=[jax.ShapeDtypeStruct((m, heads * dh), BF16),
                   jax.ShapeDtypeStruct((m // kb, heads, dh + ONES_ROWS, kb), BF16)],
        scratch_shapes=[pltpu.VMEM((k, cdim), BF16)],
        compiler_params=_cparams("arbitrary"),
        name="latent_keys_values",
    )(h, w, g.reshape(1, cdim), wuk, wuvt)


def _idx_kernel(h_ref, wk_ref, wwt_ref, g_ref, b_ref, k_ref, wt_ref, wbf_ref, *, wscale, shift):
    @pl.when(pl.program_id(0) == 0)
    def _():
        wbf_ref[...] = wk_ref[0, :, shift:shift + wbf_ref.shape[1]].astype(wbf_ref.dtype)

    h = h_ref[...]
    k = jnp.dot(h, wbf_ref[...], preferred_element_type=F32)
    mu = jnp.mean(k, axis=-1, keepdims=True)
    kc = k - mu
    var = jnp.mean(kc * kc, axis=-1, keepdims=True)
    k_ref[...] = (kc * lax.rsqrt(var + EPS) * g_ref[...] + b_ref[...]).astype(k_ref.dtype)
    wt = lax.dot_general(wwt_ref[...], h, (((1,), (1,)), ((), ())), preferred_element_type=F32)
    wt_ref[...] = wt * wscale


def _idx_small(h, w, layer, kcols, wwt, g, b, wscale):
    m, k = h.shape
    k0, di = kcols
    hi = wwt.shape[0]
    tm = _tile(m, 512)
    wspec, shift = _wwin(layer, 0, k, k0, di)
    return pl.pallas_call(
        functools.partial(_idx_kernel, wscale=wscale, shift=shift),
        grid=(m // tm,),
        in_specs=[pl.BlockSpec((tm, k), lambda i: (i, 0)), wspec,
                  pl.BlockSpec((hi, k), lambda i: (0, 0)), pl.BlockSpec((1, di), lambda i: (0, 0)),
                  pl.BlockSpec((1, di), lambda i: (0, 0))],
        out_specs=[pl.BlockSpec((tm, di), lambda i: (i, 0)), pl.BlockSpec((hi, tm), lambda i: (0, i))],
        out_shape=[jax.ShapeDtypeStruct((m, di), BF16), jax.ShapeDtypeStruct((hi, m), F32)],
        scratch_shapes=[pltpu.VMEM((k, di), BF16)],
        compiler_params=_cparams("arbitrary"),
        name="idx_small",
    )(h, w, wwt, g.reshape(1, di), b.reshape(1, di))


def _spatial_kernel(u_ref, v_ref, gb_ref, ws_ref, bst_ref, vg_ref, vb_ref, o_ref, *, chunk, groups):
    tm, bw = v_ref.shape
    e = bw // groups
    v = v_ref[...]
    mu = jnp.mean(v, axis=-1, keepdims=True)
    vc = v - mu
    var = jnp.mean(vc * vc, axis=-1, keepdims=True)
    vn = (vc * lax.rsqrt(var + EPS) * vg_ref[...] + vb_ref[...]).astype(BF16)
    row = lax.broadcasted_iota(I32, (chunk, chunk), 0)
    col = lax.broadcasted_iota(I32, (chunk, chunk), 1)
    causal = col <= row
    for g in range(groups):
        wsg = jnp.where(causal, ws_ref[g], 0.0).astype(BF16)
        bias = jnp.broadcast_to(bst_ref[:, g:g + 1], (chunk, e))
        for c in range(tm // chunk):
            rows, cols = slice(c * chunk, (c + 1) * chunk), slice(g * e, (g + 1) * e)
            mixed = jnp.dot(wsg, vn[rows, cols], preferred_element_type=F32) + bias
            o_ref[rows, cols] = (u_ref[rows, cols] * mixed * gb_ref[rows, cols]).astype(o_ref.dtype)


def _spatial(u, v, gate, w_s, b_s, vg, vb):
    m, bw = u.shape
    groups, chunk, _ = w_s.shape
    tm = 2 * chunk if m % (2 * chunk) == 0 else chunk
    row = lambda i: (i, 0)
    return pl.pallas_call(
        functools.partial(_spatial_kernel, chunk=chunk, groups=groups),
        grid=(m // tm,),
        in_specs=[pl.BlockSpec((tm, bw), row), pl.BlockSpec((tm, bw), row), pl.BlockSpec((tm, bw), row),
                  pl.BlockSpec((groups, chunk, chunk), lambda i: (0, 0, 0)),
                  pl.BlockSpec((chunk, groups), lambda i: (0, 0)),
                  pl.BlockSpec((1, bw), lambda i: (0, 0)), pl.BlockSpec((1, bw), lambda i: (0, 0))],
        out_specs=pl.BlockSpec((tm, bw), row),
        out_shape=jax.ShapeDtypeStruct((m, bw), BF16),
        compiler_params=_cparams("parallel"),
        name="spatial_gating",
    )(u, v, gate, w_s, b_s.T, vg.reshape(1, bw), vb.reshape(1, bw))


IDX_HEAD_GROUP = 4
WORD_BITS = 32
CHUNK_ROWS = WORD_BITS * V7X_SUBLANES


def _bit_transpose32(a):
    a = list(a)
    m, j = 0x0000FFFF, 16
    while j:
        k = 0
        while k < WORD_BITS:
            t = (a[k] ^ lax.shift_right_logical(a[k + j], jnp.int32(j))) & jnp.int32(m)
            a[k] = a[k] ^ t
            a[k + j] = a[k + j] ^ lax.shift_left(t, jnp.int32(j))
            k = (k + j + 1) & ~j
        j >>= 1
        m = (m ^ (m << j)) & 0xFFFFFFFF
    return a


def _attn_kernel(qi_ref, ki_ref, q_ref, qidx_ref, wt_ref, gate_ref, kidx_ref, kp_ref, vt_ref, o_ref,
                 planes_ref, alive_ref, sel_ref, qs_ref, bias_ref, acc_ref, m_ref,
                 *, qb, kb, topk, seq_bits):
    t = pl.program_id(1)
    i, kbi = qi_ref[t], ki_ref[t]
    heads, dh = vt_ref.shape[1], vt_ref.shape[2] - ONES_ROWS
    iheads, di = wt_ref.shape[0], kidx_ref.shape[1]
    cpk = kb // CHUNK_ROWS
    n_kb = ((i + 1) * qb + kb - 1) // kb
    nt = (((1,), (1,)), ((), ()))

    @pl.when(kbi == 0)
    def _select():
        for h in range(iheads):
            qs_ref[h * qb:(h + 1) * qb, :] = qidx_ref[:, h * di:(h + 1) * di]
        wt = wt_ref[...]
        qpos = i * qb + lax.broadcasted_iota(I32, (1, qb), 1)

        @pl.when(i == 0)
        def _():
            planes_ref[...] = jnp.zeros(planes_ref.shape, I32)

        def score_block(kbj, masked):
            r0 = pl.multiple_of(kbj * kb, kb)
            kblk = kidx_ref[pl.ds(r0, kb), :]
            score = jnp.zeros((kb, qb), F32)
            for hg in range(iheads // IDX_HEAD_GROUP):
                h0 = hg * IDX_HEAD_GROUP
                lt = lax.dot_general(kblk, qs_ref[h0 * qb:(h0 + IDX_HEAD_GROUP) * qb, :], nt,
                                     preferred_element_type=F32)
                for hh in range(IDX_HEAD_GROUP):
                    score = score + jnp.maximum(lt[:, hh * qb:(hh + 1) * qb], 0.0) * wt[h0 + hh:h0 + hh + 1, :]
            bits = pltpu.bitcast(score, I32)
            key = bits ^ ((bits >> 31) | INT_MIN)
            key = jnp.where(key == 0x7FFFFFFF, INT_MIN, key)
            if masked:
                kpos = r0 + lax.broadcasted_iota(I32, (kb, 1), 0)
                key = jnp.where(kpos <= qpos, key, 0)
            for cc in range(cpk):
                rows = [key[cc * CHUNK_ROWS + j * V7X_SUBLANES:cc * CHUNK_ROWS + (j + 1) * V7X_SUBLANES, :]
                        for j in range(WORD_BITS)]
                planes = _bit_transpose32(rows)
                for b in range(WORD_BITS):
                    planes_ref[b, kbj * cpk + cc] = planes[b]

        def score_body(kbj, carry):
            score_block(kbj, False)
            return carry

        lax.fori_loop(0, n_kb - 1, score_body, 0)
        score_block(n_kb - 1, True)

        def popsum(words):
            return jnp.sum(jnp.sum(lax.population_count(words), axis=0), axis=0, keepdims=True)

        cidx = lax.broadcasted_iota(I32, alive_ref.shape, 0)
        alive_ref[...] = jnp.where(cidx < n_kb * cpk, -1, 0)
        sel_ref[...] = jnp.zeros(sel_ref.shape, I32)

        def radix_body(b, k_rem):
            plane = planes_ref[b]
            alive = alive_ref[...]
            ones = alive & plane
            cnt = popsum(ones)
            take1 = cnt >= k_rem
            alive_ref[...] = jnp.where(take1, ones, alive & ~plane)
            sel_ref[...] = jnp.where(take1, sel_ref[...], sel_ref[...] | ones)
            return jnp.where(take1, k_rem, k_rem - cnt)

        need = lax.fori_loop(0, WORD_BITS, radix_body, jnp.minimum(qpos + 1, topk))

        @pl.when(jnp.max(popsum(alive_ref[...]) - need) > 0)
        def _():
            ties = alive_ref[...]
            row0 = cidx * CHUNK_ROWS + lax.broadcasted_iota(I32, alive_ref.shape, 1)

            def rows_below(lim):
                nj = jnp.clip(lax.shift_right_arithmetic(lim - row0 + (V7X_SUBLANES - 1), 3), 0, WORD_BITS)
                return jnp.where(nj <= 0, 0, lax.shift_left(jnp.int32(-1), WORD_BITS - jnp.maximum(nj, 1)))

            def idx_body(b, lim):
                cand = lim + lax.shift_left(jnp.int32(1), seq_bits - 1 - b)
                return jnp.where(popsum(ties & rows_below(cand)) < need, cand, lim)

            lim = lax.fori_loop(0, seq_bits, idx_body, jnp.zeros((1, qb), I32))
            alive_ref[...] = ties & rows_below(lim + 1)

        sel_ref[...] = sel_ref[...] | alive_ref[...]

        def bias_body(c, carry):
            words = sel_ref[c]
            for j in range(WORD_BITS):
                top = words if j == 0 else lax.shift_left(words, jnp.int32(j))
                r = pl.multiple_of(c * CHUNK_ROWS, CHUNK_ROWS) + j * V7X_SUBLANES
                bias_ref[pl.ds(r, V7X_SUBLANES), :] = jnp.where(top < 0, 0.0, MASK_NEG)
            return carry

        lax.fori_loop(0, n_kb * cpk, bias_body, 0)
        m_ref[...] = jnp.full(m_ref.shape, MASK_NEG, F32)
        acc_ref[...] = jnp.zeros(acc_ref.shape, F32)

    r0 = pl.multiple_of(kbi * kb, kb)
    bias = bias_ref[pl.ds(r0, kb), :]
    s = jnp.concatenate(
        [lax.dot_general(kp_ref[:, h * dh:(h + 1) * dh], q_ref[:, h * dh:(h + 1) * dh], nt,
                         preferred_element_type=F32) + bias for h in range(heads)], axis=1)
    m_old = m_ref[...]
    m_new = jnp.maximum(m_old, jnp.max(s, axis=0, keepdims=True))
    alpha = jnp.exp2(m_old - m_new)
    p = jnp.exp2(s - m_new).astype(vt_ref.dtype)
    m_ref[...] = m_new
    for h in range(heads):
        qc = slice(h * qb, (h + 1) * qb)
        pv = jnp.dot(vt_ref[0, h], p[:, qc], preferred_element_type=F32)
        acc_ref[h] = acc_ref[h] * alpha[:, qc] + pv

    @pl.when(kbi == n_kb - 1)
    def _finish():
        for h in range(heads):
            hc = slice(h * dh, (h + 1) * dh)
            o_t = acc_ref[h, :dh, :] * (1.0 / acc_ref[h, dh:dh + 1, :])
            o_ref[:, hc] = (o_t.T * gate_ref[:, hc]).astype(o_ref.dtype)


def _causal_steps(nq, qb, kb):
    qi, ki = [], []
    for i in range(nq):
        for k in range(((i + 1) * qb + kb - 1) // kb):
            qi.append(i)
            ki.append(k)
    return jnp.asarray(qi, I32), jnp.asarray(ki, I32)


def _sparse_attention(q, qidx, wt, gate, kidx, kproj, vt, *, batch, kb):
    m, aw = q.shape
    seq = m // batch
    heads, dh = vt.shape[1], vt.shape[2] - ONES_ROWS
    iheads, di = wt.shape[0], kidx.shape[1]
    qb = min(2 * V7X_LANES, seq)
    nq, nkb = seq // qb, seq // kb
    assert seq % kb == 0 and kb % CHUNK_ROWS == 0 and iheads % IDX_HEAD_GROUP == 0
    topk = min(TOPK_MAX, seq // 4)
    qi, ki = _causal_steps(nq, qb, kb)
    qrow = lambda b, t, qi, ki: (b * nq + qi[t], 0)
    nchunks = seq // CHUNK_ROWS
    grid_spec = pltpu.PrefetchScalarGridSpec(
        num_scalar_prefetch=2,
        grid=(batch, qi.shape[0]),
        in_specs=[pl.BlockSpec((qb, aw), qrow),
                  pl.BlockSpec((qb, iheads * di), qrow),
                  pl.BlockSpec((iheads, qb), lambda b, t, qi, ki: (0, b * nq + qi[t])),
                  pl.BlockSpec((qb, aw), qrow),
                  pl.BlockSpec((seq, di), lambda b, t, qi, ki: (b, 0), pipeline_mode=pl.Buffered(1)),
                  pl.BlockSpec((kb, aw), lambda b, t, qi, ki: (b * nkb + ki[t], 0)),
                  pl.BlockSpec((1, heads, dh + ONES_ROWS, kb), lambda b, t, qi, ki: (b * nkb + ki[t], 0, 0, 0))],
        out_specs=pl.BlockSpec((qb, aw), qrow),
        scratch_shapes=[pltpu.VMEM((WORD_BITS, nchunks, V7X_SUBLANES, qb), I32),
                        pltpu.VMEM((nchunks, V7X_SUBLANES, qb), I32),
                        pltpu.VMEM((nchunks, V7X_SUBLANES, qb), I32),
                        pltpu.VMEM((iheads * qb, di), BF16),
                        pltpu.VMEM((seq, qb), F32),
                        pltpu.VMEM((heads, dh + ONES_ROWS, qb), F32),
                        pltpu.VMEM((1, heads * qb), F32)])
    return pl.pallas_call(
        functools.partial(_attn_kernel, qb=qb, kb=kb, topk=topk, seq_bits=seq.bit_length()),
        grid_spec=grid_spec,
        out_shape=jax.ShapeDtypeStruct((m, aw), BF16),
        compiler_params=_cparams("parallel", "arbitrary"),
        name="sparse_attention",
    )(qi, ki, q, qidx, wt, gate, kidx, kproj, vt)


def _outproj_kernel(ya_ref, yb_ref, wa_ref, wb_ref, x_ref, o_ref, wabf_ref, wbbf_ref):
    @pl.when(pl.program_id(1) == 0)
    def _():
        wabf_ref[...] = wa_ref[0].astype(wabf_ref.dtype)
        wbbf_ref[...] = wb_ref[0].astype(wbbf_ref.dtype)

    acc = jnp.dot(ya_ref[...], wabf_ref[...], preferred_element_type=F32)
    acc = acc + jnp.dot(yb_ref[...], wbbf_ref[...], preferred_element_type=F32)
    o_ref[...] = x_ref[...] + acc


def _outproj(ya, yb, w, layer, x):
    m, ka = ya.shape
    kbw = yb.shape[1]
    n = w.shape[2]
    assert w.shape[1] == ka + kbw
    tm, tn = _tile(m, 1024), _tile(n, 512)
    waspec, _ = _wwin(layer, 0, ka, 0, tn, step=tn)
    wbspec, _ = _wwin(layer, ka, kbw, 0, tn, step=tn)
    return pl.pallas_call(
        _outproj_kernel,
        grid=(n // tn, m // tm),
        in_specs=[pl.BlockSpec((tm, ka), lambda j, i: (i, 0)), pl.BlockSpec((tm, kbw), lambda j, i: (i, 0)),
                  waspec, wbspec,
                  pl.BlockSpec((tm, tn), lambda j, i: (i, j))],
        out_specs=pl.BlockSpec((tm, tn), lambda j, i: (i, j)),
        out_shape=jax.ShapeDtypeStruct((m, n), F32),
        scratch_shapes=[pltpu.VMEM((ka, tn), BF16), pltpu.VMEM((kbw, tn), BF16)],
        compiler_params=_cparams("parallel", "arbitrary"),
        name="outproj",
    )(ya, yb, w, w, x)


def kernel(x, norm_g, w_in, kv_norm_g, idx_k_norm_g, idx_k_norm_b, w_uk, w_uv, v_norm_g, v_norm_b, w_s, b_s,
           w_out, final_norm_g):
    batch, seq, d = x.shape
    depth = w_in.shape[0]
    cdim, heads, dh = w_uk.shape[1:]
    aw = heads * dh
    di = idx_k_norm_g.shape[1]
    bw = v_norm_g.shape[1]
    iheads = (w_in.shape[2] - 2 * aw - cdim - di - 3 * bw) // (di + 1)
    kb = _tile(seq, 512)

    names = ("q", "c_kv", "gate_a", "q_idx", "k_idx", "w_idx", "u", "v", "gate_b")
    sizes = (aw, cdim, aw, iheads * di, di, iheads, bw, bw, bw)
    seg, off = {}, 0
    for nm, sz in zip(names, sizes):
        seg[nm] = (off, off + sz)
        off += sz

    w_widx_t = jnp.swapaxes(w_in[:, :, seg["w_idx"][0]:seg["w_idx"][1]], 1, 2).astype(BF16)
    cols = lambda nm: (seg[nm][0], seg[nm][1] - seg[nm][0])

    xf = x.reshape(batch * seq, d)
    for l in range(depth):
        h = _rmsnorm(xf, norm_g[l], BF16)
        q = _proj(h, w_in, l, cols("q"), "none", BF16, "proj_q")
        kproj, vt = _latent_keys_values(h, w_in, l, cols("c_kv"), kv_norm_g[l],
                                        w_uk[l].reshape(cdim, aw).astype(BF16),
                                        jnp.transpose(w_uv[l], (1, 2, 0)).astype(BF16), (dh ** -0.5) * LOG2E, kb)
        gate_a = _proj(h, w_in, l, cols("gate_a"), "silu", F32, "proj_gate_a")
        qidx = _proj(h, w_in, l, cols("q_idx"), "none", BF16, "proj_q_idx")
        kidx, wt = _idx_small(h, w_in, l, cols("k_idx"), w_widx_t[l], idx_k_norm_g[l], idx_k_norm_b[l],
                              (iheads ** -0.5) * (di ** -0.5))
        y_a = _sparse_attention(q, qidx, wt, gate_a, kidx, kproj, vt, batch=batch, kb=kb)
        u = _proj(h, w_in, l, cols("u"), "gelu", F32, "proj_u")
        v = _proj(h, w_in, l, cols("v"), "gelu", F32, "proj_v")
        gate_b = _proj(h, w_in, l, cols("gate_b"), "silu", F32, "proj_gate_b")
        y_b = _spatial(u, v, gate_b, w_s[l], b_s[l], v_norm_g[l], v_norm_b[l])
        xf = _outproj(y_a, y_b, w_out, l, xf)
    return _rmsnorm(xf, final_norm_g, x.dtype).reshape(batch, seq, d)
```

```python
import functools
import math

import jax
import jax.numpy as jnp
from jax import lax
from jax.experimental import pallas as pl
from jax.experimental.pallas import tpu as pltpu

F32, BF16, I32 = jnp.float32, jnp.bfloat16, jnp.int32

EPS = 1e-6
TOPK_MAX = 256
SQRT_HALF = 0.7071067811865476
LOG2E = 1.4426950408889634

V7X_LANES = 128
V7X_SUBLANES = 8
V7X_VMEM_LIMIT_BYTES = 56 * 2**20

ONES_ROWS = 16

INT_MIN = -(2**31)
MASK_NEG = -1e30


def _cparams(*sem):
    return pltpu.CompilerParams(dimension_semantics=sem, vmem_limit_bytes=V7X_VMEM_LIMIT_BYTES)


def _tile(n, want):
    t = min(n, want)
    while n % t:
        t //= 2
    return t


def _rmsnorm_kernel(x_ref, g_ref, o_ref):
    x = x_ref[...]
    ms = jnp.mean(x * x, axis=-1, keepdims=True)
    o_ref[...] = (x * lax.rsqrt(ms + EPS) * g_ref[...]).astype(o_ref.dtype)


def _rmsnorm(x, g, out_dtype):
    m, d = x.shape
    tm = _tile(m, 256)
    return pl.pallas_call(
        _rmsnorm_kernel,
        grid=(m // tm,),
        in_specs=[pl.BlockSpec((tm, d), lambda i: (i, 0)), pl.BlockSpec((1, d), lambda i: (0, 0))],
        out_specs=pl.BlockSpec((tm, d), lambda i: (i, 0)),
        out_shape=jax.ShapeDtypeStruct((m, d), out_dtype),
        compiler_params=_cparams("parallel"),
        name="rmsnorm",
    )(x, g.reshape(1, d))


def _gelu(x):
    return 0.5 * x * (1.0 + lax.erf(x * SQRT_HALF))


_ACTS = {"none": lambda x: x, "silu": jax.nn.silu, "gelu": _gelu}


def _proj_kernel(h_ref, w_ref, o_ref, *, act):
    acc = jnp.dot(h_ref[...], w_ref[...], preferred_element_type=F32)
    o_ref[...] = _ACTS[act](acc).astype(o_ref.dtype)


def _wspec(w, layer, rows, cols, index):
    return pl.BlockSpec((None, rows, cols), lambda *ids: (layer,) + tuple(index(*ids)))


def _proj(h, w, layer, cols, act, out_dtype, name):
    m, k = h.shape
    c0, n = cols
    tm, tn = _tile(m, 1024), _tile(math.gcd(n, c0), 1024)
    j0 = c0 // tn
    return pl.pallas_call(
        functools.partial(_proj_kernel, act=act),
        grid=(n // tn, m // tm),
        in_specs=[pl.BlockSpec((tm, k), lambda j, i: (i, 0)), _wspec(w, layer, k, tn, lambda j, i: (0, j0 + j))],
        out_specs=pl.BlockSpec((tm, tn), lambda j, i: (i, j)),
        out_shape=jax.ShapeDtypeStruct((m, n), out_dtype),
        compiler_params=_cparams("parallel", "parallel"),
        name=name,
    )(h, w)


def _latent_kernel(h_ref, w_ref, g_ref, wuk_ref, wuvt_ref, k_ref, vt_ref, *, kscale):
    c = jnp.dot(h_ref[...], w_ref[...], preferred_element_type=F32)
    ms = jnp.mean(c * c, axis=-1, keepdims=True)
    c = (c * lax.rsqrt(ms + EPS) * g_ref[...]).astype(BF16)
    k_ref[...] = (jnp.dot(c, wuk_ref[...], preferred_element_type=F32) * kscale).astype(k_ref.dtype)
    heads, dh, kb = wuvt_ref.shape[0], wuvt_ref.shape[1], c.shape[0]
    ones_row = (lax.broadcasted_iota(I32, (ONES_ROWS, kb), 0) == 0).astype(vt_ref.dtype)
    for hd in range(heads):
        vt = lax.dot_general(wuvt_ref[hd], c, (((1,), (1,)), ((), ())), preferred_element_type=F32)
        vt_ref[0, hd, :dh, :] = vt.astype(vt_ref.dtype)
        vt_ref[0, hd, dh:, :] = ones_row


def _latent_keys_values(h, w, layer, cols, g, wuk, wuvt, kscale, kb):
    m, k = h.shape
    c0, cdim = cols
    heads, dh, _ = wuvt.shape
    assert c0 % cdim == 0 and m % kb == 0
    return pl.pallas_call(
        functools.partial(_latent_kernel, kscale=kscale),
        grid=(m // kb,),
        in_specs=[pl.BlockSpec((kb, k), lambda i: (i, 0)), _wspec(w, layer, k, cdim, lambda i: (0, c0 // cdim)),
                  pl.BlockSpec((1, cdim), lambda i: (0, 0)), pl.BlockSpec((cdim, heads * dh), lambda i: (0, 0)),
                  pl.BlockSpec((heads, dh, cdim), lambda i: (0, 0, 0))],
        out_specs=[pl.BlockSpec((kb, heads * dh), lambda i: (i, 0)),
                   pl.BlockSpec((1, heads, dh + ONES_ROWS, kb), lambda i: (i, 0, 0, 0))],
        out_shape=[jax.ShapeDtypeStruct((m, heads * dh), BF16),
                   jax.ShapeDtypeStruct((m // kb, heads, dh + ONES_ROWS, kb), BF16)],
        compiler_params=_cparams("parallel"),
        name="latent_keys_values",
    )(h, w, g.reshape(1, cdim), wuk, wuvt)


def _idx_kernel(h_ref, wk_ref, wwt_ref, g_ref, b_ref, k_ref, wt_ref, *, wscale):
    h = h_ref[...]
    k = jnp.dot(h, wk_ref[...], preferred_element_type=F32)
    mu = jnp.mean(k, axis=-1, keepdims=True)
    kc = k - mu
    var = jnp.mean(kc * kc, axis=-1, keepdims=True)
    k_ref[...] = (kc * lax.rsqrt(var + EPS) * g_ref[...] + b_ref[...]).astype(k_ref.dtype)
    wt = lax.dot_general(wwt_ref[...], h, (((1,), (1,)), ((), ())), preferred_element_type=F32)
    wt_ref[...] = wt * wscale


def _idx_small(h, w, layer, kcols, wwt, g, b, wscale):
    m, k = h.shape
    k0, di = kcols
    hi = wwt.shape[0]
    assert k0 % di == 0
    tm = _tile(m, 512)
    return pl.pallas_call(
        functools.partial(_idx_kernel, wscale=wscale),
        grid=(m // tm,),
        in_specs=[pl.BlockSpec((tm, k), lambda i: (i, 0)), _wspec(w, layer, k, di, lambda i: (0, k0 // di)),
                  pl.BlockSpec((hi, k), lambda i: (0, 0)), pl.BlockSpec((1, di), lambda i: (0, 0)),
                  pl.BlockSpec((1, di), lambda i: (0, 0))],
        out_specs=[pl.BlockSpec((tm, di), lambda i: (i, 0)), pl.BlockSpec((hi, tm), lambda i: (0, i))],
        out_shape=[jax.ShapeDtypeStruct((m, di), BF16), jax.ShapeDtypeStruct((hi, m), F32)],
        compiler_params=_cparams("parallel"),
        name="idx_small",
    )(h, w, wwt, g.reshape(1, di), b.reshape(1, di))


def _spatial_kernel(u_ref, v_ref, gb_ref, ws_ref, bst_ref, vg_ref, vb_ref, o_ref, *, chunk, groups):
    tm, bw = v_ref.shape
    e = bw // groups
    v = v_ref[...]
    mu = jnp.mean(v, axis=-1, keepdims=True)
    vc = v - mu
    var = jnp.mean(vc * vc, axis=-1, keepdims=True)
    vn = (vc * lax.rsqrt(var + EPS) * vg_ref[...] + vb_ref[...]).astype(BF16)
    row = lax.broadcasted_iota(I32, (chunk, chunk), 0)
    col = lax.broadcasted_iota(I32, (chunk, chunk), 1)
    causal = col <= row
    for g in range(groups):
        wsg = jnp.where(causal, ws_ref[g], 0.0).astype(BF16)
        bias = jnp.broadcast_to(bst_ref[:, g:g + 1], (chunk, e))
        for c in range(tm // chunk):
            rows, cols = slice(c * chunk, (c + 1) * chunk), slice(g * e, (g + 1) * e)
            mixed = jnp.dot(wsg, vn[rows, cols], preferred_element_type=F32) + bias
            o_ref[rows, cols] = (u_ref[rows, cols] * mixed * gb_ref[rows, cols]).astype(o_ref.dtype)


def _spatial(u, v, gate, w_s, b_s, vg, vb):
    m, bw = u.shape
    groups, chunk, _ = w_s.shape
    tm = 2 * chunk if m % (2 * chunk) == 0 else chunk
    row = lambda i: (i, 0)
    return pl.pallas_call(
        functools.partial(_spatial_kernel, chunk=chunk, groups=groups),
        grid=(m // tm,),
        in_specs=[pl.BlockSpec((tm, bw), row), pl.BlockSpec((tm, bw), row), pl.BlockSpec((tm, bw), row),
                  pl.BlockSpec((groups, chunk, chunk), lambda i: (0, 0, 0)),
                  pl.BlockSpec((chunk, groups), lambda i: (0, 0)),
                  pl.BlockSpec((1, bw), lambda i: (0, 0)), pl.BlockSpec((1, bw), lambda i: (0, 0))],
        out_specs=pl.BlockSpec((tm, bw), row),
        out_shape=jax.ShapeDtypeStruct((m, bw), BF16),
        compiler_params=_cparams("parallel"),
        name="spatial_gating",
    )(u, v, gate, w_s, b_s.T, vg.reshape(1, bw), vb.reshape(1, bw))


IDX_HEAD_GROUP = 4
WORD_BITS = 32
CHUNK_ROWS = WORD_BITS * V7X_SUBLANES


def _bit_transpose32(a):
    a = list(a)
    m, j = 0x0000FFFF, 16
    while j:
        k = 0
        while k < WORD_BITS:
            t = (a[k] ^ lax.shift_right_logical(a[k + j], jnp.int32(j))) & jnp.int32(m)
            a[k] = a[k] ^ t
            a[k + j] = a[k + j] ^ lax.shift_left(t, jnp.int32(j))
            k = (k + j + 1) & ~j
        j >>= 1
        m = (m ^ (m << j)) & 0xFFFFFFFF
    return a


def _attn_kernel(qi_ref, ki_ref, q_ref, qidx_ref, wt_ref, gate_ref, kidx_ref, kp_ref, vt_ref, o_ref,
                 planes_ref, alive_ref, sel_ref, qs_ref, bias_ref, acc_ref, m_ref,
                 *, qb, kb, topk, seq_bits):
    t = pl.program_id(1)
    i, kbi = qi_ref[t], ki_ref[t]
    heads, dh = vt_ref.shape[1], vt_ref.shape[2] - ONES_ROWS
    iheads, di = wt_ref.shape[0], kidx_ref.shape[1]
    cpk = kb // CHUNK_ROWS
    n_kb = ((i + 1) * qb + kb - 1) // kb
    nt = (((1,), (1,)), ((), ()))

    @pl.when(kbi == 0)
    def _select():
        for h in range(iheads):
            qs_ref[h * qb:(h + 1) * qb, :] = qidx_ref[:, h * di:(h + 1) * di]
        wt = wt_ref[...]
        qpos = i * qb + lax.broadcasted_iota(I32, (1, qb), 1)

        @pl.when(i == 0)
        def _():
            planes_ref[...] = jnp.zeros(planes_ref.shape, I32)

        def score_block(kbj, masked):
            r0 = pl.multiple_of(kbj * kb, kb)
            kblk = kidx_ref[pl.ds(r0, kb), :]
            score = jnp.zeros((kb, qb), F32)
            for hg in range(iheads // IDX_HEAD_GROUP):
                h0 = hg * IDX_HEAD_GROUP
                lt = lax.dot_general(kblk, qs_ref[h0 * qb:(h0 + IDX_HEAD_GROUP) * qb, :], nt,
                                     preferred_element_type=F32)
                for hh in range(IDX_HEAD_GROUP):
                    score = score + jnp.maximum(lt[:, hh * qb:(hh + 1) * qb], 0.0) * wt[h0 + hh:h0 + hh + 1, :]
            bits = pltpu.bitcast(score, I32)
            key = bits ^ ((bits >> 31) | INT_MIN)
            key = jnp.where(key == 0x7FFFFFFF, INT_MIN, key)
            if masked:
                kpos = r0 + lax.broadcasted_iota(I32, (kb, 1), 0)
                key = jnp.where(kpos <= qpos, key, 0)
            for cc in range(cpk):
                rows = [key[cc * CHUNK_ROWS + j * V7X_SUBLANES:cc * CHUNK_ROWS + (j + 1) * V7X_SUBLANES, :]
                        for j in range(WORD_BITS)]
                planes = _bit_transpose32(rows)
                for b in range(WORD_BITS):
                    planes_ref[b, kbj * cpk + cc] = planes[b]

        def score_body(kbj, carry):
            score_block(kbj, False)
            return carry

        lax.fori_loop(0, n_kb - 1, score_body, 0)
        score_block(n_kb - 1, True)

        def popsum(words):
            return jnp.sum(jnp.sum(lax.population_count(words), axis=0), axis=0, keepdims=True)

        cidx = lax.broadcasted_iota(I32, alive_ref.shape, 0)
        alive_ref[...] = jnp.where(cidx < n_kb * cpk, -1, 0)
        sel_ref[...] = jnp.zeros(sel_ref.shape, I32)

        def radix_body(b, k_rem):
            plane = planes_ref[b]
            alive = alive_ref[...]
            ones = alive & plane
            cnt = popsum(ones)
            take1 = cnt >= k_rem
            alive_ref[...] = jnp.where(take1, ones, alive & ~plane)
            sel_ref[...] = jnp.where(take1, sel_ref[...], sel_ref[...] | ones)
            return jnp.where(take1, k_rem, k_rem - cnt)

        need = lax.fori_loop(0, WORD_BITS, radix_body, jnp.minimum(qpos + 1, topk))

        @pl.when(jnp.max(popsum(alive_ref[...]) - need) > 0)
        def _():
            ties = alive_ref[...]
            row0 = cidx * CHUNK_ROWS + lax.broadcasted_iota(I32, alive_ref.shape, 1)

            def rows_below(lim):
                nj = jnp.clip(lax.shift_right_arithmetic(lim - row0 + (V7X_SUBLANES - 1), 3), 0, WORD_BITS)
                return jnp.where(nj <= 0, 0, lax.shift_left(jnp.int32(-1), WORD_BITS - jnp.maximum(nj, 1)))

            def idx_body(b, lim):
                cand = lim + lax.shift_left(jnp.int32(1), seq_bits - 1 - b)
                return jnp.where(popsum(ties & rows_below(cand)) < need, cand, lim)

            lim = lax.fori_loop(0, seq_bits, idx_body, jnp.zeros((1, qb), I32))
            alive_ref[...] = ties & rows_below(lim + 1)

        sel_ref[...] = sel_ref[...] | alive_ref[...]

        def bias_body(c, carry):
            words = sel_ref[c]
            for j in range(WORD_BITS):
                top = words if j == 0 else lax.shift_left(words, jnp.int32(j))
                r = pl.multiple_of(c * CHUNK_ROWS, CHUNK_ROWS) + j * V7X_SUBLANES
                bias_ref[pl.ds(r, V7X_SUBLANES), :] = jnp.where(top < 0, 0.0, MASK_NEG)
            return carry

        lax.fori_loop(0, n_kb * cpk, bias_body, 0)
        m_ref[...] = jnp.full(m_ref.shape, MASK_NEG, F32)
        acc_ref[...] = jnp.zeros(acc_ref.shape, F32)

    r0 = pl.multiple_of(kbi * kb, kb)
    bias = bias_ref[pl.ds(r0, kb), :]
    s = jnp.concatenate(
        [lax.dot_general(kp_ref[:, h * dh:(h + 1) * dh], q_ref[:, h * dh:(h + 1) * dh], nt,
                         preferred_element_type=F32) + bias for h in range(heads)], axis=1)
    m_old = m_ref[...]
    m_new = jnp.maximum(m_old, jnp.max(s, axis=0, keepdims=True))
    alpha = jnp.exp2(m_old - m_new)
    p = jnp.exp2(s - m_new).astype(vt_ref.dtype)
    m_ref[...] = m_new
    for h in range(heads):
        qc = slice(h * qb, (h + 1) * qb)
        pv = jnp.dot(vt_ref[0, h], p[:, qc], preferred_element_type=F32)
        acc_ref[h] = acc_ref[h] * alpha[:, qc] + pv

    @pl.when(kbi == n_kb - 1)
    def _finish():
        for h in range(heads):
            hc = slice(h * dh, (h + 1) * dh)
            o_t = acc_ref[h, :dh, :] * (1.0 / acc_ref[h, dh:dh + 1, :])
            o_ref[:, hc] = (o_t.T * gate_ref[:, hc]).astype(o_ref.dtype)


def _causal_steps(nq, qb, kb):
    qi, ki = [], []
    for i in range(nq):
        for k in range(((i + 1) * qb + kb - 1) // kb):
            qi.append(i)
            ki.append(k)
    return jnp.asarray(qi, I32), jnp.asarray(ki, I32)


def _sparse_attention(q, qidx, wt, gate, kidx, kproj, vt, *, batch, kb):
    m, aw = q.shape
    seq = m // batch
    heads, dh = vt.shape[1], vt.shape[2] - ONES_ROWS
    iheads, di = wt.shape[0], kidx.shape[1]
    qb = min(2 * V7X_LANES, seq)
    nq, nkb = seq // qb, seq // kb
    assert seq % kb == 0 and kb % CHUNK_ROWS == 0 and iheads % IDX_HEAD_GROUP == 0
    topk = min(TOPK_MAX, seq // 4)
    qi, ki = _causal_steps(nq, qb, kb)
    qrow = lambda b, t, qi, ki: (b * nq + qi[t], 0)
    nchunks = seq // CHUNK_ROWS
    grid_spec = pltpu.PrefetchScalarGridSpec(
        num_scalar_prefetch=2,
        grid=(batch, qi.shape[0]),
        in_specs=[pl.BlockSpec((qb, aw), qrow),
                  pl.BlockSpec((qb, iheads * di), qrow),
                  pl.BlockSpec((iheads, qb), lambda b, t, qi, ki: (0, b * nq + qi[t])),
                  pl.BlockSpec((qb, aw), qrow),
                  pl.BlockSpec((seq, di), lambda b, t, qi, ki: (b, 0), pipeline_mode=pl.Buffered(1)),
                  pl.BlockSpec((kb, aw), lambda b, t, qi, ki: (b * nkb + ki[t], 0)),
                  pl.BlockSpec((1, heads, dh + ONES_ROWS, kb), lambda b, t, qi, ki: (b * nkb + ki[t], 0, 0, 0))],
        out_specs=pl.BlockSpec((qb, aw), qrow),
        scratch_shapes=[pltpu.VMEM((WORD_BITS, nchunks, V7X_SUBLANES, qb), I32),
                        pltpu.VMEM((nchunks, V7X_SUBLANES, qb), I32),
                        pltpu.VMEM((nchunks, V7X_SUBLANES, qb), I32),
                        pltpu.VMEM((iheads * qb, di), BF16),
                        pltpu.VMEM((seq, qb), F32),
                        pltpu.VMEM((heads, dh + ONES_ROWS, qb), F32),
                        pltpu.VMEM((1, heads * qb), F32)])
    return pl.pallas_call(
        functools.partial(_attn_kernel, qb=qb, kb=kb, topk=topk, seq_bits=seq.bit_length()),
        grid_spec=grid_spec,
        out_shape=jax.ShapeDtypeStruct((m, aw), BF16),
        compiler_params=_cparams("parallel", "arbitrary"),
        name="sparse_attention",
    )(qi, ki, q, qidx, wt, gate, kidx, kproj, vt)


def _outproj_kernel(ya_ref, yb_ref, wa_ref, wb_ref, x_ref, o_ref):
    acc = jnp.dot(ya_ref[...], wa_ref[...], preferred_element_type=F32)
    acc = acc + jnp.dot(yb_ref[...], wb_ref[...], preferred_element_type=F32)
    o_ref[...] = x_ref[...] + acc


def _outproj(ya, yb, w, layer, x):
    m, ka = ya.shape
    kbw = yb.shape[1]
    n = w.shape[2]
    assert ka == kbw and w.shape[1] == ka + kbw
    tm, tn = _tile(m, 1024), _tile(n, 1024)
    return pl.pallas_call(
        _outproj_kernel,
        grid=(n // tn, m // tm),
        in_specs=[pl.BlockSpec((tm, ka), lambda j, i: (i, 0)), pl.BlockSpec((tm, kbw), lambda j, i: (i, 0)),
                  _wspec(w, layer, ka, tn, lambda j, i: (0, j)), _wspec(w, layer, kbw, tn, lambda j, i: (1, j)),
                  pl.BlockSpec((tm, tn), lambda j, i: (i, j))],
        out_specs=pl.BlockSpec((tm, tn), lambda j, i: (i, j)),
        out_shape=jax.ShapeDtypeStruct((m, n), F32),
        compiler_params=_cparams("parallel", "parallel"),
        name="outproj",
    )(ya, yb, w, w, x)


def kernel(x, norm_g, w_in, kv_norm_g, idx_k_norm_g, idx_k_norm_b, w_uk, w_uv, v_norm_g, v_norm_b, w_s, b_s,
           w_out, final_norm_g):
    batch, seq, d = x.shape
    depth = w_in.shape[0]
    cdim, heads, dh = w_uk.shape[1:]
    aw = heads * dh
    di = idx_k_norm_g.shape[1]
    bw = v_norm_g.shape[1]
    iheads = (w_in.shape[2] - 2 * aw - cdim - di - 3 * bw) // (di + 1)
    kb = _tile(seq, 512)

    names = ("q", "c_kv", "gate_a", "q_idx", "k_idx", "w_idx", "u", "v", "gate_b")
    sizes = (aw, cdim, aw, iheads * di, di, iheads, bw, bw, bw)
    seg, off = {}, 0
    for nm, sz in zip(names, sizes):
        seg[nm] = (off, off + sz)
        off += sz

    a1, b0 = seg["k_idx"][1], seg["u"][0]
    w_in_bf = w_in[:, :, :a1].astype(BF16)
    w_b = w_in[:, :, b0:].astype(BF16)
    w_widx_t = jnp.swapaxes(w_in[:, :, seg["w_idx"][0]:seg["w_idx"][1]], 1, 2).astype(BF16)
    w_out_bf = w_out.astype(BF16)
    cols = lambda nm: (seg[nm][0], seg[nm][1] - seg[nm][0])
    bcols = lambda nm: (seg[nm][0] - b0, seg[nm][1] - seg[nm][0])

    xf = x.reshape(batch * seq, d)
    for l in range(depth):
        h = _rmsnorm(xf, norm_g[l], BF16)
        q = _proj(h, w_in_bf, l, cols("q"), "none", BF16, "proj_q")
        kproj, vt = _latent_keys_values(h, w_in_bf, l, cols("c_kv"), kv_norm_g[l],
                                        w_uk[l].reshape(cdim, aw).astype(BF16),
                                        jnp.transpose(w_uv[l], (1, 2, 0)).astype(BF16), (dh ** -0.5) * LOG2E, kb)
        gate_a = _proj(h, w_in_bf, l, cols("gate_a"), "silu", F32, "proj_gate_a")
        qidx = _proj(h, w_in_bf, l, cols("q_idx"), "none", BF16, "proj_q_idx")
        kidx, wt = _idx_small(h, w_in_bf, l, cols("k_idx"), w_widx_t[l], idx_k_norm_g[l], idx_k_norm_b[l],
                              (iheads ** -0.5) * (di ** -0.5))
        y_a = _sparse_attention(q, qidx, wt, gate_a, kidx, kproj, vt, batch=batch, kb=kb)
        u = _proj(h, w_b, l, bcols("u"), "gelu", F32, "proj_u")
        v = _proj(h, w_b, l, bcols("v"), "gelu", F32, "proj_v")
        gate_b = _proj(h, w_b, l, bcols("gate_b"), "silu", F32, "proj_gate_b")
        y_b = _spatial(u, v, gate_b, w_s[l], b_s[l], v_norm_g[l], v_norm_b[l])
        xf = _outproj(y_a, y_b, w_out_bf, l, xf)
    return _rmsnorm(xf, final_norm_g, x.dtype).reshape(batch, seq, d)
```

```python
import functools

import jax
import jax.numpy as jnp
from jax import lax
from jax.experimental import pallas as pl
from jax.experimental.pallas import tpu as pltpu

F32, BF16, I32 = jnp.float32, jnp.bfloat16, jnp.int32

EPS = 1e-6
TOPK_MAX = 256
SQRT_HALF = 0.7071067811865476
LOG2E = 1.4426950408889634

V7X_LANES = 128
V7X_SUBLANES = 8
BF16_SUBLANES = 2 * V7X_SUBLANES
V7X_VMEM_LIMIT_BYTES = 56 * 2**20

ONES_ROWS = 16

INT_MIN = -(2**31)
MASK_NEG = -1e30


def _cparams(*sem):
    return pltpu.CompilerParams(dimension_semantics=sem, vmem_limit_bytes=V7X_VMEM_LIMIT_BYTES)


def _tile(n, want):
    t = min(n, want)
    while n % t:
        t //= 2
    return t


def _rmsnorm_kernel(x_ref, g_ref, o_ref):
    x = x_ref[...]
    ms = jnp.mean(x * x, axis=-1, keepdims=True)
    o_ref[...] = (x * lax.rsqrt(ms + EPS) * g_ref[...]).astype(o_ref.dtype)


def _rmsnorm(x, g, out_dtype):
    m, d = x.shape
    tm = _tile(m, 256)
    return pl.pallas_call(
        _rmsnorm_kernel,
        grid=(m // tm,),
        in_specs=[pl.BlockSpec((tm, d), lambda i: (i, 0)), pl.BlockSpec((1, d), lambda i: (0, 0))],
        out_specs=pl.BlockSpec((tm, d), lambda i: (i, 0)),
        out_shape=jax.ShapeDtypeStruct((m, d), out_dtype),
        compiler_params=_cparams("parallel"),
        name="rmsnorm",
    )(x, g.reshape(1, d))


def _gelu(x):
    return 0.5 * x * (1.0 + lax.erf(x * SQRT_HALF))


_ACTS = {"none": lambda x: x, "silu": jax.nn.silu, "gelu": _gelu}


NT_DIMS = (((1,), (1,)), ((), ()))


def _proj_kernel(h_ref, w_ref, o_ref, *, act):
    acc = lax.dot_general(h_ref[...], w_ref[0], NT_DIMS, preferred_element_type=F32)
    o_ref[...] = _ACTS[act](acc).astype(o_ref.dtype)


def _wrows(layer, row0, rows, k, step=None):
    def index(*ids):
        r = row0 if step is None else row0 + ids[0] * step
        return (layer, pl.multiple_of(r, BF16_SUBLANES), 0)

    assert row0 % BF16_SUBLANES == 0 and (step is None or step % BF16_SUBLANES == 0)
    return pl.BlockSpec((pl.Element(1), pl.Element(rows), pl.Element(k)), index)


def _proj(h, wt, layer, cols, act, out_dtype, name):
    m, k = h.shape
    c0, n = cols
    tm, tn = _tile(m, 1024), _tile(n, 1024)
    return pl.pallas_call(
        functools.partial(_proj_kernel, act=act),
        grid=(n // tn, m // tm),
        in_specs=[pl.BlockSpec((tm, k), lambda j, i: (i, 0)), _wrows(layer, c0, tn, k, step=tn)],
        out_specs=pl.BlockSpec((tm, tn), lambda j, i: (i, j)),
        out_shape=jax.ShapeDtypeStruct((m, n), out_dtype),
        compiler_params=_cparams("parallel", "parallel"),
        name=name,
    )(h, wt)


def _latent_kernel(h_ref, w_ref, g_ref, wuk_ref, wuvt_ref, k_ref, vt_ref, *, kscale):
    c = lax.dot_general(h_ref[...], w_ref[0], NT_DIMS, preferred_element_type=F32)
    ms = jnp.mean(c * c, axis=-1, keepdims=True)
    c = (c * lax.rsqrt(ms + EPS) * g_ref[...]).astype(BF16)
    k_ref[...] = (jnp.dot(c, wuk_ref[...], preferred_element_type=F32) * kscale).astype(k_ref.dtype)
    heads, dh, kb = wuvt_ref.shape[0], wuvt_ref.shape[1], c.shape[0]
    ones_row = (lax.broadcasted_iota(I32, (ONES_ROWS, kb), 0) == 0).astype(vt_ref.dtype)
    for hd in range(heads):
        vt = lax.dot_general(wuvt_ref[hd], c, NT_DIMS, preferred_element_type=F32)
        vt_ref[0, hd, :dh, :] = vt.astype(vt_ref.dtype)
        vt_ref[0, hd, dh:, :] = ones_row


def _latent_keys_values(h, wt, layer, cols, g, wuk, wuvt, kscale, kb):
    m, k = h.shape
    c0, cdim = cols
    heads, dh, _ = wuvt.shape
    assert m % kb == 0
    return pl.pallas_call(
        functools.partial(_latent_kernel, kscale=kscale),
        grid=(m // kb,),
        in_specs=[pl.BlockSpec((kb, k), lambda i: (i, 0)), _wrows(layer, c0, cdim, k),
                  pl.BlockSpec((1, cdim), lambda i: (0, 0)), pl.BlockSpec((cdim, heads * dh), lambda i: (0, 0)),
                  pl.BlockSpec((heads, dh, cdim), lambda i: (0, 0, 0))],
        out_specs=[pl.BlockSpec((kb, heads * dh), lambda i: (i, 0)),
                   pl.BlockSpec((1, heads, dh + ONES_ROWS, kb), lambda i: (i, 0, 0, 0))],
        out_shape=[jax.ShapeDtypeStruct((m, heads * dh), BF16),
                   jax.ShapeDtypeStruct((m // kb, heads, dh + ONES_ROWS, kb), BF16)],
        compiler_params=_cparams("parallel"),
        name="latent_keys_values",
    )(h, wt, g.reshape(1, cdim), wuk, wuvt)


def _idx_kernel(h_ref, wk_ref, wwt_ref, g_ref, b_ref, k_ref, wt_ref, *, wscale):
    h = h_ref[...]
    k = lax.dot_general(h, wk_ref[0], NT_DIMS, preferred_element_type=F32)
    mu = jnp.mean(k, axis=-1, keepdims=True)
    kc = k - mu
    var = jnp.mean(kc * kc, axis=-1, keepdims=True)
    k_ref[...] = (kc * lax.rsqrt(var + EPS) * g_ref[...] + b_ref[...]).astype(k_ref.dtype)
    wt = lax.dot_general(wwt_ref[0], h, NT_DIMS, preferred_element_type=F32)
    wt_ref[...] = wt * wscale


def _idx_small(h, wt, layer, kcols, wcols, g, b, wscale):
    m, k = h.shape
    (k0, di), (w0, hi) = kcols, wcols
    tm = _tile(m, 512)
    return pl.pallas_call(
        functools.partial(_idx_kernel, wscale=wscale),
        grid=(m // tm,),
        in_specs=[pl.BlockSpec((tm, k), lambda i: (i, 0)), _wrows(layer, k0, di, k), _wrows(layer, w0, hi, k),
                  pl.BlockSpec((1, di), lambda i: (0, 0)),
                  pl.BlockSpec((1, di), lambda i: (0, 0))],
        out_specs=[pl.BlockSpec((tm, di), lambda i: (i, 0)), pl.BlockSpec((hi, tm), lambda i: (0, i))],
        out_shape=[jax.ShapeDtypeStruct((m, di), BF16), jax.ShapeDtypeStruct((hi, m), F32)],
        compiler_params=_cparams("parallel"),
        name="idx_small",
    )(h, wt, wt, g.reshape(1, di), b.reshape(1, di))


def _spatial_kernel(u_ref, v_ref, gb_ref, ws_ref, bst_ref, vg_ref, vb_ref, o_ref, *, chunk, groups):
    tm, bw = v_ref.shape
    e = bw // groups
    v = v_ref[...]
    mu = jnp.mean(v, axis=-1, keepdims=True)
    vc = v - mu
    var = jnp.mean(vc * vc, axis=-1, keepdims=True)
    vn = (vc * lax.rsqrt(var + EPS) * vg_ref[...] + vb_ref[...]).astype(BF16)
    row = lax.broadcasted_iota(I32, (chunk, chunk), 0)
    col = lax.broadcasted_iota(I32, (chunk, chunk), 1)
    causal = col <= row
    for g in range(groups):
        wsg = jnp.where(causal, ws_ref[g], 0.0).astype(BF16)
        bias = jnp.broadcast_to(bst_ref[:, g:g + 1], (chunk, e))
        for c in range(tm // chunk):
            rows, cols = slice(c * chunk, (c + 1) * chunk), slice(g * e, (g + 1) * e)
            mixed = jnp.dot(wsg, vn[rows, cols], preferred_element_type=F32) + bias
            o_ref[rows, cols] = (u_ref[rows, cols] * mixed * gb_ref[rows, cols]).astype(o_ref.dtype)


def _spatial(u, v, gate, w_s, b_s, vg, vb):
    m, bw = u.shape
    groups, chunk, _ = w_s.shape
    tm = 2 * chunk if m % (2 * chunk) == 0 else chunk
    row = lambda i: (i, 0)
    return pl.pallas_call(
        functools.partial(_spatial_kernel, chunk=chunk, groups=groups),
        grid=(m // tm,),
        in_specs=[pl.BlockSpec((tm, bw), row), pl.BlockSpec((tm, bw), row), pl.BlockSpec((tm, bw), row),
                  pl.BlockSpec((groups, chunk, chunk), lambda i: (0, 0, 0)),
                  pl.BlockSpec((chunk, groups), lambda i: (0, 0)),
                  pl.BlockSpec((1, bw), lambda i: (0, 0)), pl.BlockSpec((1, bw), lambda i: (0, 0))],
        out_specs=pl.BlockSpec((tm, bw), row),
        out_shape=jax.ShapeDtypeStruct((m, bw), BF16),
        compiler_params=_cparams("parallel"),
        name="spatial_gating",
    )(u, v, gate, w_s, b_s.T, vg.reshape(1, bw), vb.reshape(1, bw))


IDX_HEAD_GROUP = 4
WORD_BITS = 32
CHUNK_ROWS = WORD_BITS * V7X_SUBLANES


def _bit_transpose32(a):
    a = list(a)
    m, j = 0x0000FFFF, 16
    while j:
        k = 0
        while k < WORD_BITS:
            t = (a[k] ^ lax.shift_right_logical(a[k + j], jnp.int32(j))) & jnp.int32(m)
            a[k] = a[k] ^ t
            a[k + j] = a[k + j] ^ lax.shift_left(t, jnp.int32(j))
            k = (k + j + 1) & ~j
        j >>= 1
        m = (m ^ (m << j)) & 0xFFFFFFFF
    return a


def _attn_kernel(qi_ref, ki_ref, q_ref, qidx_ref, wt_ref, gate_ref, kidx_ref, kp_ref, vt_ref, o_ref,
                 planes_ref, alive_ref, sel_ref, qs_ref, bias_ref, acc_ref, m_ref,
                 *, qb, kb, topk, seq_bits):
    t = pl.program_id(1)
    i, kbi = qi_ref[t], ki_ref[t]
    heads, dh = vt_ref.shape[1], vt_ref.shape[2] - ONES_ROWS
    iheads, di = wt_ref.shape[0], kidx_ref.shape[1]
    cpk = kb // CHUNK_ROWS
    n_kb = ((i + 1) * qb + kb - 1) // kb
    nt = NT_DIMS

    @pl.when(kbi == 0)
    def _select():
        for h in range(iheads):
            qs_ref[h * qb:(h + 1) * qb, :] = qidx_ref[:, h * di:(h + 1) * di]
        wt = wt_ref[...]
        qpos = i * qb + lax.broadcasted_iota(I32, (1, qb), 1)

        @pl.when(i == 0)
        def _():
            planes_ref[...] = jnp.zeros(planes_ref.shape, I32)

        def score_block(kbj, masked):
            r0 = pl.multiple_of(kbj * kb, kb)
            kblk = kidx_ref[pl.ds(r0, kb), :]
            score = jnp.zeros((kb, qb), F32)
            for hg in range(iheads // IDX_HEAD_GROUP):
                h0 = hg * IDX_HEAD_GROUP
                lt = lax.dot_general(kblk, qs_ref[h0 * qb:(h0 + IDX_HEAD_GROUP) * qb, :], nt,
                                     preferred_element_type=F32)
                for hh in range(IDX_HEAD_GROUP):
                    score = score + jnp.maximum(lt[:, hh * qb:(hh + 1) * qb], 0.0) * wt[h0 + hh:h0 + hh + 1, :]
            bits = pltpu.bitcast(score, I32)
            key = bits ^ ((bits >> 31) | INT_MIN)
            key = jnp.where(key == 0x7FFFFFFF, INT_MIN, key)
            if masked:
                kpos = r0 + lax.broadcasted_iota(I32, (kb, 1), 0)
                key = jnp.where(kpos <= qpos, key, 0)
            for cc in range(cpk):
                rows = [key[cc * CHUNK_ROWS + j * V7X_SUBLANES:cc * CHUNK_ROWS + (j + 1) * V7X_SUBLANES, :]
                        for j in range(WORD_BITS)]
                planes = _bit_transpose32(rows)
                for b in range(WORD_BITS):
                    planes_ref[b, kbj * cpk + cc] = planes[b]

        def score_body(kbj, carry):
            score_block(kbj, False)
            return carry

        lax.fori_loop(0, n_kb - 1, score_body, 0)
        score_block(n_kb - 1, True)

        def popsum(words):
            return jnp.sum(jnp.sum(lax.population_count(words), axis=0), axis=0, keepdims=True)

        cidx = lax.broadcasted_iota(I32, alive_ref.shape, 0)
        alive_ref[...] = jnp.where(cidx < n_kb * cpk, -1, 0)
        sel_ref[...] = jnp.zeros(sel_ref.shape, I32)

        def radix_body(b, k_rem):
            plane = planes_ref[b]
            alive = alive_ref[...]
            ones = alive & plane
            cnt = popsum(ones)
            take1 = cnt >= k_rem
            alive_ref[...] = jnp.where(take1, ones, alive & ~plane)
            sel_ref[...] = jnp.where(take1, sel_ref[...], sel_ref[...] | ones)
            return jnp.where(take1, k_rem, k_rem - cnt)

        need = lax.fori_loop(0, WORD_BITS, radix_body, jnp.minimum(qpos + 1, topk))

        @pl.when(jnp.max(popsum(alive_ref[...]) - need) > 0)
        def _():
            ties = alive_ref[...]
            row0 = cidx * CHUNK_ROWS + lax.broadcasted_iota(I32, alive_ref.shape, 1)

            def rows_below(lim):
                nj = jnp.clip(lax.shift_right_arithmetic(lim - row0 + (V7X_SUBLANES - 1), 3), 0, WORD_BITS)
                return jnp.where(nj <= 0, 0, lax.shift_left(jnp.int32(-1), WORD_BITS - jnp.maximum(nj, 1)))

            def idx_body(b, lim):
                cand = lim + lax.shift_left(jnp.int32(1), seq_bits - 1 - b)
                return jnp.where(popsum(ties & rows_below(cand)) < need, cand, lim)

            lim = lax.fori_loop(0, seq_bits, idx_body, jnp.zeros((1, qb), I32))
            alive_ref[...] = ties & rows_below(lim + 1)

        sel_ref[...] = sel_ref[...] | alive_ref[...]

        def bias_body(c, carry):
            words = sel_ref[c]
            for j in range(WORD_BITS):
                top = words if j == 0 else lax.shift_left(words, jnp.int32(j))
                r = pl.multiple_of(c * CHUNK_ROWS, CHUNK_ROWS) + j * V7X_SUBLANES
                bias_ref[pl.ds(r, V7X_SUBLANES), :] = jnp.where(top < 0, 0.0, MASK_NEG)
            return carry

        lax.fori_loop(0, n_kb * cpk, bias_body, 0)
        m_ref[...] = jnp.full(m_ref.shape, MASK_NEG, F32)
        acc_ref[...] = jnp.zeros(acc_ref.shape, F32)

    r0 = pl.multiple_of(kbi * kb, kb)
    bias = bias_ref[pl.ds(r0, kb), :]
    s = jnp.concatenate(
        [lax.dot_general(kp_ref[:, h * dh:(h + 1) * dh], q_ref[:, h * dh:(h + 1) * dh], nt,
                         preferred_element_type=F32) + bias for h in range(heads)], axis=1)
    m_old = m_ref[...]
    m_new = jnp.maximum(m_old, jnp.max(s, axis=0, keepdims=True))
    alpha = jnp.exp2(m_old - m_new)
    p = jnp.exp2(s - m_new).astype(vt_ref.dtype)
    m_ref[...] = m_new
    for h in range(heads):
        qc = slice(h * qb, (h + 1) * qb)
        pv = jnp.dot(vt_ref[0, h], p[:, qc], preferred_element_type=F32)
        acc_ref[h] = acc_ref[h] * alpha[:, qc] + pv

    @pl.when(kbi == n_kb - 1)
    def _finish():
        for h in range(heads):
            hc = slice(h * dh, (h + 1) * dh)
            o_t = acc_ref[h, :dh, :] * (1.0 / acc_ref[h, dh:dh + 1, :])
            o_ref[:, hc] = (o_t.T * gate_ref[:, hc]).astype(o_ref.dtype)


def _causal_steps(nq, qb, kb):
    qi, ki = [], []
    for i in range(nq):
        for k in range(((i + 1) * qb + kb - 1) // kb):
            qi.append(i)
            ki.append(k)
    return jnp.asarray(qi, I32), jnp.asarray(ki, I32)


def _sparse_attention(q, qidx, wt, gate, kidx, kproj, vt, *, batch, kb):
    m, aw = q.shape
    seq = m // batch
    heads, dh = vt.shape[1], vt.shape[2] - ONES_ROWS
    iheads, di = wt.shape[0], kidx.shape[1]
    qb = min(2 * V7X_LANES, seq)
    nq, nkb = seq // qb, seq // kb
    assert seq % kb == 0 and kb % CHUNK_ROWS == 0 and iheads % IDX_HEAD_GROUP == 0
    topk = min(TOPK_MAX, seq // 4)
    qi, ki = _causal_steps(nq, qb, kb)
    qrow = lambda b, t, qi, ki: (b * nq + qi[t], 0)
    nchunks = seq // CHUNK_ROWS
    grid_spec = pltpu.PrefetchScalarGridSpec(
        num_scalar_prefetch=2,
        grid=(batch, qi.shape[0]),
        in_specs=[pl.BlockSpec((qb, aw), qrow),
                  pl.BlockSpec((qb, iheads * di), qrow),
                  pl.BlockSpec((iheads, qb), lambda b, t, qi, ki: (0, b * nq + qi[t])),
                  pl.BlockSpec((qb, aw), qrow),
                  pl.BlockSpec((seq, di), lambda b, t, qi, ki: (b, 0), pipeline_mode=pl.Buffered(1)),
                  pl.BlockSpec((kb, aw), lambda b, t, qi, ki: (b * nkb + ki[t], 0)),
                  pl.BlockSpec((1, heads, dh + ONES_ROWS, kb), lambda b, t, qi, ki: (b * nkb + ki[t], 0, 0, 0))],
        out_specs=pl.BlockSpec((qb, aw), qrow),
        scratch_shapes=[pltpu.VMEM((WORD_BITS, nchunks, V7X_SUBLANES, qb), I32),
                        pltpu.VMEM((nchunks, V7X_SUBLANES, qb), I32),
                        pltpu.VMEM((nchunks, V7X_SUBLANES, qb), I32),
                        pltpu.VMEM((iheads * qb, di), BF16),
                        pltpu.VMEM((seq, qb), F32),
                        pltpu.VMEM((heads, dh + ONES_ROWS, qb), F32),
                        pltpu.VMEM((1, heads * qb), F32)])
    return pl.pallas_call(
        functools.partial(_attn_kernel, qb=qb, kb=kb, topk=topk, seq_bits=seq.bit_length()),
        grid_spec=grid_spec,
        out_shape=jax.ShapeDtypeStruct((m, aw), BF16),
        compiler_params=_cparams("parallel", "arbitrary"),
        name="sparse_attention",
    )(qi, ki, q, qidx, wt, gate, kidx, kproj, vt)


def _outproj_kernel(ya_ref, yb_ref, wa_ref, wb_ref, x_ref, o_ref):
    acc = jnp.dot(ya_ref[...], wa_ref[...], preferred_element_type=F32)
    acc = acc + jnp.dot(yb_ref[...], wb_ref[...], preferred_element_type=F32)
    o_ref[...] = x_ref[...] + acc


def _wspec(w, layer, rows, cols, index):
    return pl.BlockSpec((None, rows, cols), lambda *ids: (layer,) + tuple(index(*ids)))


def _outproj(ya, yb, w, layer, x):
    m, ka = ya.shape
    kbw = yb.shape[1]
    n = w.shape[2]
    assert ka == kbw and w.shape[1] == ka + kbw
    tm, tn = _tile(m, 1024), _tile(n, 1024)
    return pl.pallas_call(
        _outproj_kernel,
        grid=(n // tn, m // tm),
        in_specs=[pl.BlockSpec((tm, ka), lambda j, i: (i, 0)), pl.BlockSpec((tm, kbw), lambda j, i: (i, 0)),
                  _wspec(w, layer, ka, tn, lambda j, i: (0, j)), _wspec(w, layer, kbw, tn, lambda j, i: (1, j)),
                  pl.BlockSpec((tm, tn), lambda j, i: (i, j))],
        out_specs=pl.BlockSpec((tm, tn), lambda j, i: (i, j)),
        out_shape=jax.ShapeDtypeStruct((m, n), F32),
        compiler_params=_cparams("parallel", "parallel"),
        name="outproj",
    )(ya, yb, w, w, x)


def kernel(x, norm_g, w_in, kv_norm_g, idx_k_norm_g, idx_k_norm_b, w_uk, w_uv, v_norm_g, v_norm_b, w_s, b_s,
           w_out, final_norm_g):
    batch, seq, d = x.shape
    depth = w_in.shape[0]
    cdim, heads, dh = w_uk.shape[1:]
    aw = heads * dh
    di = idx_k_norm_g.shape[1]
    bw = v_norm_g.shape[1]
    iheads = (w_in.shape[2] - 2 * aw - cdim - di - 3 * bw) // (di + 1)
    kb = _tile(seq, 512)

    names = ("q", "c_kv", "gate_a", "q_idx", "k_idx", "w_idx", "u", "v", "gate_b")
    sizes = (aw, cdim, aw, iheads * di, di, iheads, bw, bw, bw)
    seg, off = {}, 0
    for nm, sz in zip(names, sizes):
        seg[nm] = (off, off + sz)
        off += sz

    w_in_t = jnp.swapaxes(w_in, 1, 2).astype(BF16)
    w_out_bf = w_out.astype(BF16)
    cols = lambda nm: (seg[nm][0], seg[nm][1] - seg[nm][0])

    xf = x.reshape(batch * seq, d)
    for l in range(depth):
        h = _rmsnorm(xf, norm_g[l], BF16)
        q = _proj(h, w_in_t, l, cols("q"), "none", BF16, "proj_q")
        kproj, vt = _latent_keys_values(h, w_in_t, l, cols("c_kv"), kv_norm_g[l],
                                        w_uk[l].reshape(cdim, aw).astype(BF16),
                                        jnp.transpose(w_uv[l], (1, 2, 0)).astype(BF16), (dh ** -0.5) * LOG2E, kb)
        gate_a = _proj(h, w_in_t, l, cols("gate_a"), "silu", F32, "proj_gate_a")
        qidx = _proj(h, w_in_t, l, cols("q_idx"), "none", BF16, "proj_q_idx")
        kidx, wt = _idx_small(h, w_in_t, l, cols("k_idx"), cols("w_idx"), idx_k_norm_g[l], idx_k_norm_b[l],
                              (iheads ** -0.5) * (di ** -0.5))
        y_a = _sparse_attention(q, qidx, wt, gate_a, kidx, kproj, vt, batch=batch, kb=kb)
        u = _proj(h, w_in_t, l, cols("u"), "gelu", F32, "proj_u")
        v = _proj(h, w_in_t, l, cols("v"), "gelu", F32, "proj_v")
        gate_b = _proj(h, w_in_t, l, cols("gate_b"), "silu", F32, "proj_gate_b")
        y_b = _spatial(u, v, gate_b, w_s[l], b_s[l], v_norm_g[l], v_norm_b[l])
        xf = _outproj(y_a, y_b, w_out_bf, l, xf)
    return _rmsnorm(xf, final_norm_g, x.dtype).reshape(batch, seq, d)
```

```python
import functools

import jax
import jax.numpy as jnp
from jax import lax
from jax.experimental import pallas as pl
from jax.experimental.pallas import tpu as pltpu

F32, BF16, I32 = jnp.float32, jnp.bfloat16, jnp.int32

EPS = 1e-6
TOPK_MAX = 256
SQRT_HALF = 0.7071067811865476
LOG2E = 1.4426950408889634

V7X_LANES = 128
V7X_SUBLANES = 8
BF16_SUBLANES = 2 * V7X_SUBLANES
V7X_VMEM_LIMIT_BYTES = 56 * 2**20

ONES_ROWS = 16

INT_MIN = -(2**31)
MASK_NEG = -1e30


def _cparams(*sem):
    return pltpu.CompilerParams(dimension_semantics=sem, vmem_limit_bytes=V7X_VMEM_LIMIT_BYTES)


def _tile(n, want):
    t = min(n, want)
    while n % t:
        t //= 2
    return t


def _rmsnorm_kernel(x_ref, g_ref, o_ref):
    x = x_ref[...]
    ms = jnp.mean(x * x, axis=-1, keepdims=True)
    o_ref[...] = (x * lax.rsqrt(ms + EPS) * g_ref[...]).astype(o_ref.dtype)


def _rmsnorm(x, g, out_dtype):
    m, d = x.shape
    tm = _tile(m, 512)
    return pl.pallas_call(
        _rmsnorm_kernel,
        grid=(m // tm,),
        in_specs=[pl.BlockSpec((tm, d), lambda i: (i, 0)), pl.BlockSpec((1, d), lambda i: (0, 0))],
        out_specs=pl.BlockSpec((tm, d), lambda i: (i, 0)),
        out_shape=jax.ShapeDtypeStruct((m, d), out_dtype),
        compiler_params=_cparams("parallel"),
        name="rmsnorm",
    )(x, g.reshape(1, d))


def _gelu(x):
    return 0.5 * x * (1.0 + lax.erf(x * SQRT_HALF))


_ACTS = {"none": lambda x: x, "silu": jax.nn.silu, "gelu": _gelu}


NT_DIMS = (((1,), (1,)), ((), ()))


def _proj_kernel(h_ref, w_ref, o_ref, *, act):
    acc = lax.dot_general(h_ref[...], w_ref[0], NT_DIMS, preferred_element_type=F32)
    o_ref[...] = _ACTS[act](acc).astype(o_ref.dtype)


def _wrows(layer, row0, rows, k, step=None):
    def index(*ids):
        r = row0 if step is None else row0 + ids[0] * step
        return (layer, pl.multiple_of(r, BF16_SUBLANES), 0)

    assert row0 % BF16_SUBLANES == 0 and (step is None or step % BF16_SUBLANES == 0)
    return pl.BlockSpec((pl.Element(1), pl.Element(rows), pl.Element(k)), index)


def _proj(h, wt, layer, cols, act, out_dtype, name):
    m, k = h.shape
    c0, n = cols
    tm, tn = _tile(m, 1024), _tile(n, 1024)
    return pl.pallas_call(
        functools.partial(_proj_kernel, act=act),
        grid=(n // tn, m // tm),
        in_specs=[pl.BlockSpec((tm, k), lambda j, i: (i, 0)), _wrows(layer, c0, tn, k, step=tn)],
        out_specs=pl.BlockSpec((tm, tn), lambda j, i: (i, j)),
        out_shape=jax.ShapeDtypeStruct((m, n), out_dtype),
        compiler_params=_cparams("parallel", "parallel"),
        name=name,
    )(h, wt)


def _latent_kernel(h_ref, w_ref, g_ref, wuk_ref, wuvt_ref, k_ref, vt_ref, *, kscale):
    c = lax.dot_general(h_ref[...], w_ref[0], NT_DIMS, preferred_element_type=F32)
    ms = jnp.mean(c * c, axis=-1, keepdims=True)
    c = (c * lax.rsqrt(ms + EPS) * g_ref[...]).astype(BF16)
    k_ref[...] = (jnp.dot(c, wuk_ref[...], preferred_element_type=F32) * kscale).astype(k_ref.dtype)
    heads, dh, kb = wuvt_ref.shape[0], wuvt_ref.shape[1], c.shape[0]
    ones_row = (lax.broadcasted_iota(I32, (ONES_ROWS, kb), 0) == 0).astype(vt_ref.dtype)
    for hd in range(heads):
        vt = lax.dot_general(wuvt_ref[hd], c, NT_DIMS, preferred_element_type=F32)
        vt_ref[0, hd, :dh, :] = vt.astype(vt_ref.dtype)
        vt_ref[0, hd, dh:, :] = ones_row


def _latent_keys_values(h, wt, layer, cols, g, wuk, wuvt, kscale, kb):
    m, k = h.shape
    c0, cdim = cols
    heads, dh, _ = wuvt.shape
    assert m % kb == 0
    return pl.pallas_call(
        functools.partial(_latent_kernel, kscale=kscale),
        grid=(m // kb,),
        in_specs=[pl.BlockSpec((kb, k), lambda i: (i, 0)), _wrows(layer, c0, cdim, k),
                  pl.BlockSpec((1, cdim), lambda i: (0, 0)), pl.BlockSpec((cdim, heads * dh), lambda i: (0, 0)),
                  pl.BlockSpec((heads, dh, cdim), lambda i: (0, 0, 0))],
        out_specs=[pl.BlockSpec((kb, heads * dh), lambda i: (i, 0)),
                   pl.BlockSpec((1, heads, dh + ONES_ROWS, kb), lambda i: (i, 0, 0, 0))],
        out_shape=[jax.ShapeDtypeStruct((m, heads * dh), BF16),
                   jax.ShapeDtypeStruct((m // kb, heads, dh + ONES_ROWS, kb), BF16)],
        compiler_params=_cparams("parallel"),
        name="latent_keys_values",
    )(h, wt, g.reshape(1, cdim), wuk, wuvt)


def _idx_kernel(h_ref, wk_ref, wwt_ref, g_ref, b_ref, k_ref, wt_ref, *, wscale):
    h = h_ref[...]
    k = lax.dot_general(h, wk_ref[0], NT_DIMS, preferred_element_type=F32)
    mu = jnp.mean(k, axis=-1, keepdims=True)
    kc = k - mu
    var = jnp.mean(kc * kc, axis=-1, keepdims=True)
    k_ref[...] = (kc * lax.rsqrt(var + EPS) * g_ref[...] + b_ref[...]).astype(k_ref.dtype)
    wt = lax.dot_general(wwt_ref[0], h, NT_DIMS, preferred_element_type=F32)
    wt_ref[...] = wt * wscale


def _idx_small(h, wt, layer, kcols, wcols, g, b, wscale):
    m, k = h.shape
    (k0, di), (w0, hi) = kcols, wcols
    tm = _tile(m, 512)
    return pl.pallas_call(
        functools.partial(_idx_kernel, wscale=wscale),
        grid=(m // tm,),
        in_specs=[pl.BlockSpec((tm, k), lambda i: (i, 0)), _wrows(layer, k0, di, k), _wrows(layer, w0, hi, k),
                  pl.BlockSpec((1, di), lambda i: (0, 0)),
                  pl.BlockSpec((1, di), lambda i: (0, 0))],
        out_specs=[pl.BlockSpec((tm, di), lambda i: (i, 0)), pl.BlockSpec((hi, tm), lambda i: (0, i))],
        out_shape=[jax.ShapeDtypeStruct((m, di), BF16), jax.ShapeDtypeStruct((hi, m), F32)],
        compiler_params=_cparams("parallel"),
        name="idx_small",
    )(h, wt, wt, g.reshape(1, di), b.reshape(1, di))


def _spatial_kernel(u_ref, v_ref, gb_ref, ws_ref, bst_ref, vg_ref, vb_ref, o_ref, *, chunk, groups):
    tm, bw = v_ref.shape
    e = bw // groups
    v = v_ref[...]
    mu = jnp.mean(v, axis=-1, keepdims=True)
    vc = v - mu
    var = jnp.mean(vc * vc, axis=-1, keepdims=True)
    vn = (vc * lax.rsqrt(var + EPS) * vg_ref[...] + vb_ref[...]).astype(BF16)
    row = lax.broadcasted_iota(I32, (chunk, chunk), 0)
    col = lax.broadcasted_iota(I32, (chunk, chunk), 1)
    causal = col <= row
    for g in range(groups):
        wsg = jnp.where(causal, ws_ref[g], 0.0).astype(BF16)
        bias = jnp.broadcast_to(bst_ref[:, g:g + 1], (chunk, e))
        for c in range(tm // chunk):
            rows, cols = slice(c * chunk, (c + 1) * chunk), slice(g * e, (g + 1) * e)
            mixed = jnp.dot(wsg, vn[rows, cols], preferred_element_type=F32) + bias
            o_ref[rows, cols] = (u_ref[rows, cols] * mixed * gb_ref[rows, cols]).astype(o_ref.dtype)


def _spatial(u, v, gate, w_s, b_s, vg, vb):
    m, bw = u.shape
    groups, chunk, _ = w_s.shape
    tm = 2 * chunk if m % (2 * chunk) == 0 else chunk
    row = lambda i: (i, 0)
    return pl.pallas_call(
        functools.partial(_spatial_kernel, chunk=chunk, groups=groups),
        grid=(m // tm,),
        in_specs=[pl.BlockSpec((tm, bw), row), pl.BlockSpec((tm, bw), row), pl.BlockSpec((tm, bw), row),
                  pl.BlockSpec((groups, chunk, chunk), lambda i: (0, 0, 0)),
                  pl.BlockSpec((chunk, groups), lambda i: (0, 0)),
                  pl.BlockSpec((1, bw), lambda i: (0, 0)), pl.BlockSpec((1, bw), lambda i: (0, 0))],
        out_specs=pl.BlockSpec((tm, bw), row),
        out_shape=jax.ShapeDtypeStruct((m, bw), BF16),
        compiler_params=_cparams("parallel"),
        name="spatial_gating",
    )(u, v, gate, w_s, b_s.T, vg.reshape(1, bw), vb.reshape(1, bw))


KB_PER_STEP = 2
WORD_BITS = 32
CHUNK_ROWS = WORD_BITS * V7X_SUBLANES


def _bit_transpose32(a):
    a = list(a)
    m, j = 0x0000FFFF, 16
    while j:
        k = 0
        while k < WORD_BITS:
            t = (a[k] ^ lax.shift_right_logical(a[k + j], jnp.int32(j))) & jnp.int32(m)
            a[k] = a[k] ^ t
            a[k + j] = a[k + j] ^ lax.shift_left(t, jnp.int32(j))
            k = (k + j + 1) & ~j
        j >>= 1
        m = (m ^ (m << j)) & 0xFFFFFFFF
    return a


def _attn_kernel(qi_ref, ki_ref, q_ref, qidx_ref, wt_ref, gate_ref, kidx_ref, kp_ref, vt_ref, o_ref,
                 planes_ref, alive_ref, sel_ref, bias_ref, acc_ref, m_ref,
                 *, qb, kb, topk, seq_bits):
    t = pl.program_id(1)
    i, kg = qi_ref[t], ki_ref[t]
    heads, dh = vt_ref.shape[1], vt_ref.shape[2] - ONES_ROWS
    iheads, di = wt_ref.shape[0], kidx_ref.shape[1]
    cpk = kb // CHUNK_ROWS
    n_kb = ((i + 1) * qb + kb - 1) // kb
    nt = NT_DIMS

    @pl.when(kg == 0)
    def _select():
        wt = wt_ref[...]
        qpos = i * qb + lax.broadcasted_iota(I32, (1, qb), 1)

        @pl.when(i == 0)
        def _():
            planes_ref[...] = jnp.zeros(planes_ref.shape, I32)

        def score_block(kbj, masked):
            r0 = pl.multiple_of(kbj * kb, kb)
            kblk = kidx_ref[pl.ds(r0, kb), :]
            score = jnp.zeros((kb, qb), F32)
            for h in range(iheads):
                lt = lax.dot_general(kblk, qidx_ref[:, h * di:(h + 1) * di], nt, preferred_element_type=F32)
                score = score + jnp.maximum(lt, 0.0) * wt[h:h + 1, :]
            bits = pltpu.bitcast(score, I32)
            key = bits ^ ((bits >> 31) | INT_MIN)
            key = jnp.where(key == 0x7FFFFFFF, INT_MIN, key)
            if masked:
                kpos = r0 + lax.broadcasted_iota(I32, (kb, 1), 0)
                key = jnp.where(kpos <= qpos, key, 0)
            for cc in range(cpk):
                rows = [key[cc * CHUNK_ROWS + j * V7X_SUBLANES:cc * CHUNK_ROWS + (j + 1) * V7X_SUBLANES, :]
                        for j in range(WORD_BITS)]
                planes = _bit_transpose32(rows)
                for b in range(WORD_BITS):
                    planes_ref[b, kbj * cpk + cc] = planes[b]

        def score_body(kbj, carry):
            score_block(kbj, False)
            return carry

        lax.fori_loop(0, n_kb - 1, score_body, 0)
        score_block(n_kb - 1, True)

        def popsum(words):
            return jnp.sum(jnp.sum(lax.population_count(words), axis=0), axis=0, keepdims=True)

        cidx = lax.broadcasted_iota(I32, alive_ref.shape, 0)
        alive_ref[...] = jnp.where(cidx < n_kb * cpk, -1, 0)
        sel_ref[...] = jnp.zeros(sel_ref.shape, I32)

        def radix_body(b, k_rem):
            plane = planes_ref[b]
            alive = alive_ref[...]
            ones = alive & plane
            cnt = popsum(ones)
            take1 = cnt >= k_rem
            alive_ref[...] = jnp.where(take1, ones, alive & ~plane)
            sel_ref[...] = jnp.where(take1, sel_ref[...], sel_ref[...] | ones)
            return jnp.where(take1, k_rem, k_rem - cnt)

        need = lax.fori_loop(0, WORD_BITS, radix_body, jnp.minimum(qpos + 1, topk))

        @pl.when(jnp.max(popsum(alive_ref[...]) - need) > 0)
        def _():
            ties = alive_ref[...]
            row0 = cidx * CHUNK_ROWS + lax.broadcasted_iota(I32, alive_ref.shape, 1)

            def rows_below(lim):
                nj = jnp.clip(lax.shift_right_arithmetic(lim - row0 + (V7X_SUBLANES - 1), 3), 0, WORD_BITS)
                return jnp.where(nj <= 0, 0, lax.shift_left(jnp.int32(-1), WORD_BITS - jnp.maximum(nj, 1)))

            def idx_body(b, lim):
                cand = lim + lax.shift_left(jnp.int32(1), seq_bits - 1 - b)
                return jnp.where(popsum(ties & rows_below(cand)) < need, cand, lim)

            lim = lax.fori_loop(0, seq_bits, idx_body, jnp.zeros((1, qb), I32))
            alive_ref[...] = ties & rows_below(lim + 1)

        sel_ref[...] = sel_ref[...] | alive_ref[...]

        def bias_body(c, carry):
            words = sel_ref[c]
            rows8 = []
            for j in range(WORD_BITS):
                top = words if j == 0 else lax.shift_left(words, jnp.int32(j))
                rows8.append(jnp.where(top < 0, 0.0, MASK_NEG))
            for j in range(0, WORD_BITS, 2):
                r = pl.multiple_of(c * CHUNK_ROWS, CHUNK_ROWS) + j * V7X_SUBLANES
                bias_ref[pl.ds(r, BF16_SUBLANES), :] = jnp.concatenate(rows8[j:j + 2], axis=0).astype(bias_ref.dtype)
            return carry

        lax.fori_loop(0, n_kb * cpk, bias_body, 0)
        m_ref[...] = jnp.full(m_ref.shape, MASK_NEG, F32)
        acc_ref[...] = jnp.zeros(acc_ref.shape, F32)

    def block_body(sub, carry):
        r0 = pl.multiple_of((kg * KB_PER_STEP + sub) * kb, kb)
        rk = pl.multiple_of(sub * kb, kb)
        bias = bias_ref[pl.ds(r0, kb), :].astype(F32)
        s = jnp.concatenate(
            [lax.dot_general(kp_ref[pl.ds(rk, kb), h * dh:(h + 1) * dh], q_ref[:, h * dh:(h + 1) * dh], nt,
                             preferred_element_type=F32) + bias for h in range(heads)], axis=1)
        m_old = m_ref[...]
        m_new = jnp.maximum(m_old, jnp.max(s, axis=0, keepdims=True))
        alpha = jnp.exp2(m_old - m_new)
        p = jnp.exp2(s - m_new).astype(vt_ref.dtype)
        m_ref[...] = m_new
        for h in range(heads):
            qc = slice(h * qb, (h + 1) * qb)
            pv = jnp.dot(vt_ref[sub, h], p[:, qc], preferred_element_type=F32)
            acc_ref[h] = acc_ref[h] * alpha[:, qc] + pv
        return carry

    lax.fori_loop(0, jnp.minimum(KB_PER_STEP, n_kb - kg * KB_PER_STEP), block_body, 0)

    @pl.when((kg + 1) * KB_PER_STEP >= n_kb)
    def _finish():
        for h in range(heads):
            hc = slice(h * dh, (h + 1) * dh)
            o_t = acc_ref[h, :dh, :] * (1.0 / acc_ref[h, dh:dh + 1, :])
            o_ref[:, hc] = (o_t.T * gate_ref[:, hc]).astype(o_ref.dtype)


def _causal_steps(nq, qb, kb):
    qi, ki = [], []
    for i in range(nq):
        n_kb = ((i + 1) * qb + kb - 1) // kb
        for k in range((n_kb + KB_PER_STEP - 1) // KB_PER_STEP):
            qi.append(i)
            ki.append(k)
    return jnp.asarray(qi, I32), jnp.asarray(ki, I32)


def _sparse_attention(q, qidx, wt, gate, kidx, kproj, vt, *, batch, kb):
    m, aw = q.shape
    seq = m // batch
    heads, dh = vt.shape[1], vt.shape[2] - ONES_ROWS
    iheads, di = wt.shape[0], kidx.shape[1]
    qb = min(2 * V7X_LANES, seq)
    nq, nkg = seq // qb, seq // (KB_PER_STEP * kb)
    assert seq % (KB_PER_STEP * kb) == 0 and kb % CHUNK_ROWS == 0
    topk = min(TOPK_MAX, seq // 4)
    qi, ki = _causal_steps(nq, qb, kb)
    qrow = lambda b, t, qi, ki: (b * nq + qi[t], 0)
    nchunks = seq // CHUNK_ROWS
    grid_spec = pltpu.PrefetchScalarGridSpec(
        num_scalar_prefetch=2,
        grid=(batch, qi.shape[0]),
        in_specs=[pl.BlockSpec((qb, aw), qrow),
                  pl.BlockSpec((qb, iheads * di), qrow),
                  pl.BlockSpec((iheads, qb), lambda b, t, qi, ki: (0, b * nq + qi[t])),
                  pl.BlockSpec((qb, aw), qrow),
                  pl.BlockSpec((seq, di), lambda b, t, qi, ki: (b, 0), pipeline_mode=pl.Buffered(1)),
                  pl.BlockSpec((KB_PER_STEP * kb, aw), lambda b, t, qi, ki: (b * nkg + ki[t], 0)),
                  pl.BlockSpec((KB_PER_STEP, heads, dh + ONES_ROWS, kb), lambda b, t, qi, ki: (b * nkg + ki[t], 0, 0, 0))],
        out_specs=pl.BlockSpec((qb, aw), qrow),
        scratch_shapes=[pltpu.VMEM((WORD_BITS, nchunks, V7X_SUBLANES, qb), I32),
                        pltpu.VMEM((nchunks, V7X_SUBLANES, qb), I32),
                        pltpu.VMEM((nchunks, V7X_SUBLANES, qb), I32),
                        pltpu.VMEM((seq, qb), BF16),
                        pltpu.VMEM((heads, dh + ONES_ROWS, qb), F32),
                        pltpu.VMEM((1, heads * qb), F32)])
    return pl.pallas_call(
        functools.partial(_attn_kernel, qb=qb, kb=kb, topk=topk, seq_bits=seq.bit_length()),
        grid_spec=grid_spec,
        out_shape=jax.ShapeDtypeStruct((m, aw), BF16),
        compiler_params=_cparams("parallel", "arbitrary"),
        name="sparse_attention",
    )(qi, ki, q, qidx, wt, gate, kidx, kproj, vt)


def _outproj_kernel(ya_ref, yb_ref, wa_ref, wb_ref, x_ref, o_ref):
    acc = jnp.dot(ya_ref[...], wa_ref[...], preferred_element_type=F32)
    acc = acc + jnp.dot(yb_ref[...], wb_ref[...], preferred_element_type=F32)
    o_ref[...] = x_ref[...] + acc


def _wspec(w, layer, rows, cols, index):
    return pl.BlockSpec((None, rows, cols), lambda *ids: (layer,) + tuple(index(*ids)))


def _outproj(ya, yb, w, layer, x):
    m, ka = ya.shape
    kbw = yb.shape[1]
    n = w.shape[2]
    assert ka == kbw and w.shape[1] == ka + kbw
    tm, tn = _tile(m, 1024), _tile(n, 1024)
    return pl.pallas_call(
        _outproj_kernel,
        grid=(n // tn, m // tm),
        in_specs=[pl.BlockSpec((tm, ka), lambda j, i: (i, 0)), pl.BlockSpec((tm, kbw), lambda j, i: (i, 0)),
                  _wspec(w, layer, ka, tn, lambda j, i: (0, j)), _wspec(w, layer, kbw, tn, lambda j, i: (1, j)),
                  pl.BlockSpec((tm, tn), lambda j, i: (i, j))],
        out_specs=pl.BlockSpec((tm, tn), lambda j, i: (i, j)),
        out_shape=jax.ShapeDtypeStruct((m, n), F32),
        compiler_params=_cparams("parallel", "parallel"),
        name="outproj",
    )(ya, yb, w, w, x)


def kernel(x, norm_g, w_in, kv_norm_g, idx_k_norm_g, idx_k_norm_b, w_uk, w_uv, v_norm_g, v_norm_b, w_s, b_s,
           w_out, final_norm_g):
    batch, seq, d = x.shape
    depth = w_in.shape[0]
    cdim, heads, dh = w_uk.shape[1:]
    aw = heads * dh
    di = idx_k_norm_g.shape[1]
    bw = v_norm_g.shape[1]
    iheads = (w_in.shape[2] - 2 * aw - cdim - di - 3 * bw) // (di + 1)
    kb = _tile(seq, 512)

    names = ("q", "c_kv", "gate_a", "q_idx", "k_idx", "w_idx", "u", "v", "gate_b")
    sizes = (aw, cdim, aw, iheads * di, di, iheads, bw, bw, bw)
    seg, off = {}, 0
    for nm, sz in zip(names, sizes):
        seg[nm] = (off, off + sz)
        off += sz

    w_in_t = jnp.swapaxes(w_in, 1, 2).astype(BF16)
    w_out_bf = w_out.astype(BF16)
    cols = lambda nm: (seg[nm][0], seg[nm][1] - seg[nm][0])

    xf = x.reshape(batch * seq, d)
    for l in range(depth):
        h = _rmsnorm(xf, norm_g[l], BF16)
        q = _proj(h, w_in_t, l, cols("q"), "none", BF16, "proj_q")
        kproj, vt = _latent_keys_values(h, w_in_t, l, cols("c_kv"), kv_norm_g[l],
                                        w_uk[l].reshape(cdim, aw).astype(BF16),
                                        jnp.transpose(w_uv[l], (1, 2, 0)).astype(BF16), (dh ** -0.5) * LOG2E, kb)
        gate_a = _proj(h, w_in_t, l, cols("gate_a"), "silu", F32, "proj_gate_a")
        qidx = _proj(h, w_in_t, l, cols("q_idx"), "none", BF16, "proj_q_idx")
        kidx, wt = _idx_small(h, w_in_t, l, cols("k_idx"), cols("w_idx"), idx_k_norm_g[l], idx_k_norm_b[l],
                              (iheads ** -0.5) * (di ** -0.5))
        y_a = _sparse_attention(q, qidx, wt, gate_a, kidx, kproj, vt, batch=batch, kb=kb)
        u = _proj(h, w_in_t, l, cols("u"), "gelu", F32, "proj_u")
        v = _proj(h, w_in_t, l, cols("v"), "gelu", F32, "proj_v")
        gate_b = _proj(h, w_in_t, l, cols("gate_b"), "silu", F32, "proj_gate_b")
        y_b = _spatial(u, v, gate_b, w_s[l], b_s[l], v_norm_g[l], v_norm_b[l])
        xf = _outproj(y_a, y_b, w_out_bf, l, xf)
    return _rmsnorm(xf, final_norm_g, x.dtype).reshape(batch, seq, d)
```

```python
import functools

import jax
import jax.numpy as jnp
from jax import lax
from jax.experimental import pallas as pl
from jax.experimental.pallas import tpu as pltpu

F32, BF16, I32 = jnp.float32, jnp.bfloat16, jnp.int32

EPS = 1e-6
TOPK_MAX = 256
SQRT_HALF = 0.7071067811865476
LOG2E = 1.4426950408889634

V7X_LANES = 128
V7X_SUBLANES = 8
BF16_SUBLANES = 2 * V7X_SUBLANES
V7X_VMEM_LIMIT_BYTES = 56 * 2**20

ONES_ROWS = 16

INT_MIN = -(2**31)
MASK_NEG = -1e30


def _cparams(*sem):
    return pltpu.CompilerParams(dimension_semantics=sem, vmem_limit_bytes=V7X_VMEM_LIMIT_BYTES)


def _tile(n, want):
    t = min(n, want)
    while n % t:
        t //= 2
    return t


def _rmsnorm_kernel(x_ref, g_ref, o_ref):
    x = x_ref[...]
    ms = jnp.mean(x * x, axis=-1, keepdims=True)
    o_ref[...] = (x * lax.rsqrt(ms + EPS) * g_ref[...]).astype(o_ref.dtype)


def _rmsnorm(x, g, out_dtype):
    m, d = x.shape
    tm = _tile(m, 512)
    return pl.pallas_call(
        _rmsnorm_kernel,
        grid=(m // tm,),
        in_specs=[pl.BlockSpec((tm, d), lambda i: (i, 0)), pl.BlockSpec((1, d), lambda i: (0, 0))],
        out_specs=pl.BlockSpec((tm, d), lambda i: (i, 0)),
        out_shape=jax.ShapeDtypeStruct((m, d), out_dtype),
        compiler_params=_cparams("parallel"),
        name="rmsnorm",
    )(x, g.reshape(1, d))


def _gelu(x):
    return 0.5 * x * (1.0 + lax.erf(x * SQRT_HALF))


_ACTS = {"none": lambda x: x, "silu": jax.nn.silu, "gelu": _gelu}


NT_DIMS = (((1,), (1,)), ((), ()))


def _proj_kernel(h_ref, w_ref, o_ref, *, act):
    acc = lax.dot_general(h_ref[...], w_ref[0], NT_DIMS, preferred_element_type=F32)
    o_ref[...] = _ACTS[act](acc).astype(o_ref.dtype)


def _wrows(layer, row0, rows, k, step=None):
    def index(*ids):
        r = row0 if step is None else row0 + ids[0] * step
        return (layer, pl.multiple_of(r, BF16_SUBLANES), 0)

    assert row0 % BF16_SUBLANES == 0 and (step is None or step % BF16_SUBLANES == 0)
    return pl.BlockSpec((pl.Element(1), pl.Element(rows), pl.Element(k)), index)


def _proj(h, wt, layer, cols, act, out_dtype, name):
    m, k = h.shape
    c0, n = cols
    tm, tn = _tile(m, 1024), _tile(n, 1024)
    return pl.pallas_call(
        functools.partial(_proj_kernel, act=act),
        grid=(n // tn, m // tm),
        in_specs=[pl.BlockSpec((tm, k), lambda j, i: (i, 0)), _wrows(layer, c0, tn, k, step=tn)],
        out_specs=pl.BlockSpec((tm, tn), lambda j, i: (i, j)),
        out_shape=jax.ShapeDtypeStruct((m, n), out_dtype),
        compiler_params=_cparams("parallel", "parallel"),
        name=name,
    )(h, wt)


def _latent_kernel(h_ref, w_ref, g_ref, wuk_ref, wuvt_ref, k_ref, vt_ref, *, kscale):
    c = lax.dot_general(h_ref[...], w_ref[0], NT_DIMS, preferred_element_type=F32)
    ms = jnp.mean(c * c, axis=-1, keepdims=True)
    c = (c * lax.rsqrt(ms + EPS) * g_ref[...]).astype(BF16)
    k_ref[...] = (jnp.dot(c, wuk_ref[...], preferred_element_type=F32) * kscale).astype(k_ref.dtype)
    heads, dh, kb = vt_ref.shape[1], vt_ref.shape[2] - ONES_ROWS, c.shape[0]
    vt = lax.dot_general(wuvt_ref[...], c, NT_DIMS, preferred_element_type=F32).astype(vt_ref.dtype)
    ones_row = (lax.broadcasted_iota(I32, (ONES_ROWS, kb), 0) == 0).astype(vt_ref.dtype)
    for hd in range(heads):
        vt_ref[0, hd, :dh, :] = vt[hd * dh:(hd + 1) * dh, :]
        vt_ref[0, hd, dh:, :] = ones_row


def _latent_keys_values(h, wt, layer, cols, g, wuk, wuvt, kscale, kb):
    m, k = h.shape
    c0, cdim = cols
    heads, dh, _ = wuvt.shape
    assert m % kb == 0
    return pl.pallas_call(
        functools.partial(_latent_kernel, kscale=kscale),
        grid=(m // kb,),
        in_specs=[pl.BlockSpec((kb, k), lambda i: (i, 0)), _wrows(layer, c0, cdim, k),
                  pl.BlockSpec((1, cdim), lambda i: (0, 0)), pl.BlockSpec((cdim, heads * dh), lambda i: (0, 0)),
                  pl.BlockSpec((heads * dh, cdim), lambda i: (0, 0))],
        out_specs=[pl.BlockSpec((kb, heads * dh), lambda i: (i, 0)),
                   pl.BlockSpec((1, heads, dh + ONES_ROWS, kb), lambda i: (i, 0, 0, 0))],
        out_shape=[jax.ShapeDtypeStruct((m, heads * dh), BF16),
                   jax.ShapeDtypeStruct((m // kb, heads, dh + ONES_ROWS, kb), BF16)],
        compiler_params=_cparams("parallel"),
        name="latent_keys_values",
    )(h, wt, g.reshape(1, cdim), wuk, wuvt.reshape(heads * dh, cdim))


def _idx_kernel(h_ref, w_ref, g_ref, b_ref, k_ref, wt_ref, *, wscale):
    di, hi = k_ref.shape[1], wt_ref.shape[0]
    r = lax.dot_general(h_ref[...], w_ref[0], NT_DIMS, preferred_element_type=F32)
    k = r[:, :di]
    mu = jnp.mean(k, axis=-1, keepdims=True)
    kc = k - mu
    var = jnp.mean(kc * kc, axis=-1, keepdims=True)
    k_ref[...] = (kc * lax.rsqrt(var + EPS) * g_ref[...] + b_ref[...]).astype(k_ref.dtype)
    wt_ref[...] = r[:, di:].T[:hi, :] * wscale


def _idx_small(h, wt, layer, kcols, wcols, g, b, wscale):
    m, k = h.shape
    (k0, di), (w0, hi) = kcols, wcols
    assert w0 == k0 + di and hi <= di
    tm = _tile(m, 512)
    return pl.pallas_call(
        functools.partial(_idx_kernel, wscale=wscale),
        grid=(m // tm,),
        in_specs=[pl.BlockSpec((tm, k), lambda i: (i, 0)), _wrows(layer, k0, 2 * di, k),
                  pl.BlockSpec((1, di), lambda i: (0, 0)),
                  pl.BlockSpec((1, di), lambda i: (0, 0))],
        out_specs=[pl.BlockSpec((tm, di), lambda i: (i, 0)), pl.BlockSpec((hi, tm), lambda i: (0, i))],
        out_shape=[jax.ShapeDtypeStruct((m, di), BF16), jax.ShapeDtypeStruct((hi, m), F32)],
        compiler_params=_cparams("parallel"),
        name="idx_small",
    )(h, wt, g.reshape(1, di), b.reshape(1, di))


def _spatial_kernel(u_ref, v_ref, gb_ref, ws_ref, bst_ref, vg_ref, vb_ref, o_ref, *, chunk, groups):
    tm, bw = v_ref.shape
    e = bw // groups
    v = v_ref[...]
    mu = jnp.mean(v, axis=-1, keepdims=True)
    vc = v - mu
    var = jnp.mean(vc * vc, axis=-1, keepdims=True)
    vn = (vc * lax.rsqrt(var + EPS) * vg_ref[...] + vb_ref[...]).astype(BF16)
    row = lax.broadcasted_iota(I32, (chunk, chunk), 0)
    col = lax.broadcasted_iota(I32, (chunk, chunk), 1)
    causal = col <= row
    for g in range(groups):
        wsg = jnp.where(causal, ws_ref[g], 0.0).astype(BF16)
        bias = jnp.broadcast_to(bst_ref[:, g:g + 1], (chunk, e))
        for c in range(tm // chunk):
            rows, cols = slice(c * chunk, (c + 1) * chunk), slice(g * e, (g + 1) * e)
            mixed = jnp.dot(wsg, vn[rows, cols], preferred_element_type=F32) + bias
            o_ref[rows, cols] = (u_ref[rows, cols] * mixed * gb_ref[rows, cols]).astype(o_ref.dtype)


def _spatial(u, v, gate, w_s, b_s, vg, vb):
    m, bw = u.shape
    groups, chunk, _ = w_s.shape
    tm = 2 * chunk if m % (2 * chunk) == 0 else chunk
    row = lambda i: (i, 0)
    return pl.pallas_call(
        functools.partial(_spatial_kernel, chunk=chunk, groups=groups),
        grid=(m // tm,),
        in_specs=[pl.BlockSpec((tm, bw), row), pl.BlockSpec((tm, bw), row), pl.BlockSpec((tm, bw), row),
                  pl.BlockSpec((groups, chunk, chunk), lambda i: (0, 0, 0)),
                  pl.BlockSpec((chunk, groups), lambda i: (0, 0)),
                  pl.BlockSpec((1, bw), lambda i: (0, 0)), pl.BlockSpec((1, bw), lambda i: (0, 0))],
        out_specs=pl.BlockSpec((tm, bw), row),
        out_shape=jax.ShapeDtypeStruct((m, bw), BF16),
        compiler_params=_cparams("parallel"),
        name="spatial_gating",
    )(u, v, gate, w_s, b_s.T, vg.reshape(1, bw), vb.reshape(1, bw))


KB_PER_STEP = 2
WORD_BITS = 32
CHUNK_ROWS = WORD_BITS * V7X_SUBLANES


def _bit_transpose32(a):
    a = list(a)
    m, j = 0x0000FFFF, 16
    while j:
        k = 0
        while k < WORD_BITS:
            t = (a[k] ^ lax.shift_right_logical(a[k + j], jnp.int32(j))) & jnp.int32(m)
            a[k] = a[k] ^ t
            a[k + j] = a[k + j] ^ lax.shift_left(t, jnp.int32(j))
            k = (k + j + 1) & ~j
        j >>= 1
        m = (m ^ (m << j)) & 0xFFFFFFFF
    return a


def _attn_kernel(qi_ref, ki_ref, q_ref, qidx_ref, wt_ref, gate_ref, kidx_ref, kp_ref, vt_ref, o_ref,
                 planes_ref, alive_ref, sel_ref, bias_ref, acc_ref, m_ref,
                 *, qb, kb, topk, seq_bits):
    t = pl.program_id(1)
    i, kg = qi_ref[t], ki_ref[t]
    heads, dh = vt_ref.shape[1], vt_ref.shape[2] - ONES_ROWS
    iheads, di = wt_ref.shape[0], kidx_ref.shape[1]
    cpk = kb // CHUNK_ROWS
    n_kb = ((i + 1) * qb + kb - 1) // kb
    nt = NT_DIMS

    @pl.when(kg == 0)
    def _select():
        wt = wt_ref[...]
        qpos = i * qb + lax.broadcasted_iota(I32, (1, qb), 1)

        @pl.when(i == 0)
        def _():
            planes_ref[...] = jnp.zeros(planes_ref.shape, I32)

        def score_block(kbj, masked):
            r0 = pl.multiple_of(kbj * kb, kb)
            kblk = kidx_ref[pl.ds(r0, kb), :]
            score = jnp.zeros((kb, qb), F32)
            for h in range(iheads):
                lt = lax.dot_general(kblk, qidx_ref[:, h * di:(h + 1) * di], nt, preferred_element_type=F32)
                score = score + jnp.maximum(lt, 0.0) * wt[h:h + 1, :]
            bits = pltpu.bitcast(score, I32)
            key = bits ^ ((bits >> 31) | INT_MIN)
            key = jnp.where(key == 0x7FFFFFFF, INT_MIN, key)
            if masked:
                kpos = r0 + lax.broadcasted_iota(I32, (kb, 1), 0)
                key = jnp.where(kpos <= qpos, key, 0)
            for cc in range(cpk):
                rows = [key[cc * CHUNK_ROWS + j * V7X_SUBLANES:cc * CHUNK_ROWS + (j + 1) * V7X_SUBLANES, :]
                        for j in range(WORD_BITS)]
                planes = _bit_transpose32(rows)
                for b in range(WORD_BITS):
                    planes_ref[b, kbj * cpk + cc] = planes[b]

        def score_body(kbj, carry):
            score_block(kbj, False)
            return carry

        lax.fori_loop(0, n_kb - 1, score_body, 0)
        score_block(n_kb - 1, True)

        def popsum(words):
            return jnp.sum(jnp.sum(lax.population_count(words), axis=0), axis=0, keepdims=True)

        cidx = lax.broadcasted_iota(I32, alive_ref.shape, 0)
        alive_ref[...] = jnp.where(cidx < n_kb * cpk, -1, 0)
        sel_ref[...] = jnp.zeros(sel_ref.shape, I32)

        def radix_body(b, k_rem):
            plane = planes_ref[b]
            alive = alive_ref[...]
            ones = alive & plane
            cnt = popsum(ones)
            take1 = cnt >= k_rem
            alive_ref[...] = jnp.where(take1, ones, alive & ~plane)
            sel_ref[...] = jnp.where(take1, sel_ref[...], sel_ref[...] | ones)
            return jnp.where(take1, k_rem, k_rem - cnt)

        need = lax.fori_loop(0, WORD_BITS, radix_body, jnp.minimum(qpos + 1, topk))

        @pl.when(jnp.max(popsum(alive_ref[...]) - need) > 0)
        def _():
            ties = alive_ref[...]
            row0 = cidx * CHUNK_ROWS + lax.broadcasted_iota(I32, alive_ref.shape, 1)

            def rows_below(lim):
                nj = jnp.clip(lax.shift_right_arithmetic(lim - row0 + (V7X_SUBLANES - 1), 3), 0, WORD_BITS)
                return jnp.where(nj <= 0, 0, lax.shift_left(jnp.int32(-1), WORD_BITS - jnp.maximum(nj, 1)))

            def idx_body(b, lim):
                cand = lim + lax.shift_left(jnp.int32(1), seq_bits - 1 - b)
                return jnp.where(popsum(ties & rows_below(cand)) < need, cand, lim)

            lim = lax.fori_loop(0, seq_bits, idx_body, jnp.zeros((1, qb), I32))
            alive_ref[...] = ties & rows_below(lim + 1)

        sel_ref[...] = sel_ref[...] | alive_ref[...]

        def bias_body(c, carry):
            words = sel_ref[c]
            rows8 = []
            for j in range(WORD_BITS):
                top = words if j == 0 else lax.shift_left(words, jnp.int32(j))
                rows8.append(jnp.where(top < 0, 0.0, MASK_NEG))
            for j in range(0, WORD_BITS, 2):
                r = pl.multiple_of(c * CHUNK_ROWS, CHUNK_ROWS) + j * V7X_SUBLANES
                bias_ref[pl.ds(r, BF16_SUBLANES), :] = jnp.concatenate(rows8[j:j + 2], axis=0).astype(bias_ref.dtype)
            return carry

        lax.fori_loop(0, n_kb * cpk, bias_body, 0)
        m_ref[...] = jnp.full(m_ref.shape, MASK_NEG, F32)
        acc_ref[...] = jnp.zeros(acc_ref.shape, F32)

    def block_body(sub, carry):
        r0 = pl.multiple_of((kg * KB_PER_STEP + sub) * kb, kb)
        rk = pl.multiple_of(sub * kb, kb)
        bias = bias_ref[pl.ds(r0, kb), :].astype(F32)
        s = jnp.concatenate(
            [lax.dot_general(kp_ref[pl.ds(rk, kb), h * dh:(h + 1) * dh], q_ref[:, h * dh:(h + 1) * dh], nt,
                             preferred_element_type=F32) + bias for h in range(heads)], axis=1)
        m_old = m_ref[...]
        m_new = jnp.maximum(m_old, jnp.max(s, axis=0, keepdims=True))
        alpha = jnp.exp2(m_old - m_new)
        p = jnp.exp2(s - m_new).astype(vt_ref.dtype)
        m_ref[...] = m_new
        for h in range(heads):
            qc = slice(h * qb, (h + 1) * qb)
            pv = jnp.dot(vt_ref[sub, h], p[:, qc], preferred_element_type=F32)
            acc_ref[h] = acc_ref[h] * alpha[:, qc] + pv
        return carry

    lax.fori_loop(0, jnp.minimum(KB_PER_STEP, n_kb - kg * KB_PER_STEP), block_body, 0)

    @pl.when((kg + 1) * KB_PER_STEP >= n_kb)
    def _finish():
        for h in range(heads):
            hc = slice(h * dh, (h + 1) * dh)
            o_t = acc_ref[h, :dh, :] * (1.0 / acc_ref[h, dh:dh + 1, :])
            o_ref[:, hc] = (o_t.T * gate_ref[:, hc]).astype(o_ref.dtype)


def _causal_steps(nq, qb, kb):
    qi, ki = [], []
    for i in range(nq):
        n_kb = ((i + 1) * qb + kb - 1) // kb
        for k in range((n_kb + KB_PER_STEP - 1) // KB_PER_STEP):
            qi.append(i)
            ki.append(k)
    return jnp.asarray(qi, I32), jnp.asarray(ki, I32)


def _sparse_attention(q, qidx, wt, gate, kidx, kproj, vt, *, batch, kb):
    m, aw = q.shape
    seq = m // batch
    heads, dh = vt.shape[1], vt.shape[2] - ONES_ROWS
    iheads, di = wt.shape[0], kidx.shape[1]
    qb = min(2 * V7X_LANES, seq)
    nq, nkg = seq // qb, seq // (KB_PER_STEP * kb)
    assert seq % (KB_PER_STEP * kb) == 0 and kb % CHUNK_ROWS == 0
    topk = min(TOPK_MAX, seq // 4)
    qi, ki = _causal_steps(nq, qb, kb)
    qrow = lambda b, t, qi, ki: (b * nq + qi[t], 0)
    nchunks = seq // CHUNK_ROWS
    grid_spec = pltpu.PrefetchScalarGridSpec(
        num_scalar_prefetch=2,
        grid=(batch, qi.shape[0]),
        in_specs=[pl.BlockSpec((qb, aw), qrow),
                  pl.BlockSpec((qb, iheads * di), qrow),
                  pl.BlockSpec((iheads, qb), lambda b, t, qi, ki: (0, b * nq + qi[t])),
                  pl.BlockSpec((qb, aw), qrow),
                  pl.BlockSpec((seq, di), lambda b, t, qi, ki: (b, 0), pipeline_mode=pl.Buffered(1)),
                  pl.BlockSpec((KB_PER_STEP * kb, aw), lambda b, t, qi, ki: (b * nkg + ki[t], 0)),
                  pl.BlockSpec((KB_PER_STEP, heads, dh + ONES_ROWS, kb), lambda b, t, qi, ki: (b * nkg + ki[t], 0, 0, 0))],
        out_specs=pl.BlockSpec((qb, aw), qrow),
        scratch_shapes=[pltpu.VMEM((WORD_BITS, nchunks, V7X_SUBLANES, qb), I32),
                        pltpu.VMEM((nchunks, V7X_SUBLANES, qb), I32),
                        pltpu.VMEM((nchunks, V7X_SUBLANES, qb), I32),
                        pltpu.VMEM((seq, qb), BF16),
                        pltpu.VMEM((heads, dh + ONES_ROWS, qb), F32),
                        pltpu.VMEM((1, heads * qb), F32)])
    return pl.pallas_call(
        functools.partial(_attn_kernel, qb=qb, kb=kb, topk=topk, seq_bits=seq.bit_length()),
        grid_spec=grid_spec,
        out_shape=jax.ShapeDtypeStruct((m, aw), BF16),
        compiler_params=_cparams("parallel", "arbitrary"),
        name="sparse_attention",
    )(qi, ki, q, qidx, wt, gate, kidx, kproj, vt)


def _outproj_kernel(ya_ref, yb_ref, wa_ref, wb_ref, x_ref, o_ref):
    acc = jnp.dot(ya_ref[...], wa_ref[...], preferred_element_type=F32)
    acc = acc + jnp.dot(yb_ref[...], wb_ref[...], preferred_element_type=F32)
    o_ref[...] = x_ref[...] + acc


def _wspec(w, layer, rows, cols, index):
    return pl.BlockSpec((None, rows, cols), lambda *ids: (layer,) + tuple(index(*ids)))


def _outproj(ya, yb, w, layer, x):
    m, ka = ya.shape
    kbw = yb.shape[1]
    n = w.shape[2]
    assert ka == kbw and w.shape[1] == ka + kbw
    tm, tn = _tile(m, 1024), _tile(n, 1024)
    return pl.pallas_call(
        _outproj_kernel,
        grid=(n // tn, m // tm),
        in_specs=[pl.BlockSpec((tm, ka), lambda j, i: (i, 0)), pl.BlockSpec((tm, kbw), lambda j, i: (i, 0)),
                  _wspec(w, layer, ka, tn, lambda j, i: (0, j)), _wspec(w, layer, kbw, tn, lambda j, i: (1, j)),
                  pl.BlockSpec((tm, tn), lambda j, i: (i, j))],
        out_specs=pl.BlockSpec((tm, tn), lambda j, i: (i, j)),
        out_shape=jax.ShapeDtypeStruct((m, n), F32),
        compiler_params=_cparams("parallel", "parallel"),
        name="outproj",
    )(ya, yb, w, w, x)


def kernel(x, norm_g, w_in, kv_norm_g, idx_k_norm_g, idx_k_norm_b, w_uk, w_uv, v_norm_g, v_norm_b, w_s, b_s,
           w_out, final_norm_g):
    batch, seq, d = x.shape
    depth = w_in.shape[0]
    cdim, heads, dh = w_uk.shape[1:]
    aw = heads * dh
    di = idx_k_norm_g.shape[1]
    bw = v_norm_g.shape[1]
    iheads = (w_in.shape[2] - 2 * aw - cdim - di - 3 * bw) // (di + 1)
    kb = _tile(seq, 512)

    names = ("q", "c_kv", "gate_a", "q_idx", "k_idx", "w_idx", "u", "v", "gate_b")
    sizes = (aw, cdim, aw, iheads * di, di, iheads, bw, bw, bw)
    seg, off = {}, 0
    for nm, sz in zip(names, sizes):
        seg[nm] = (off, off + sz)
        off += sz

    w_in_t = jnp.swapaxes(w_in, 1, 2).astype(BF16)
    w_out_bf = w_out.astype(BF16)
    cols = lambda nm: (seg[nm][0], seg[nm][1] - seg[nm][0])

    xf = x.reshape(batch * seq, d)
    for l in range(depth):
        h = _rmsnorm(xf, norm_g[l], BF16)
        q = _proj(h, w_in_t, l, cols("q"), "none", BF16, "proj_q")
        kproj, vt = _latent_keys_values(h, w_in_t, l, cols("c_kv"), kv_norm_g[l],
                                        w_uk[l].reshape(cdim, aw).astype(BF16),
                                        jnp.transpose(w_uv[l], (1, 2, 0)).astype(BF16), (dh ** -0.5) * LOG2E, kb)
        gate_a = _proj(h, w_in_t, l, cols("gate_a"), "silu", F32, "proj_gate_a")
        qidx = _proj(h, w_in_t, l, cols("q_idx"), "none", BF16, "proj_q_idx")
        kidx, wt = _idx_small(h, w_in_t, l, cols("k_idx"), cols("w_idx"), idx_k_norm_g[l], idx_k_norm_b[l],
                              (iheads ** -0.5) * (di ** -0.5))
        y_a = _sparse_attention(q, qidx, wt, gate_a, kidx, kproj, vt, batch=batch, kb=kb)
        u = _proj(h, w_in_t, l, cols("u"), "gelu", F32, "proj_u")
        v = _proj(h, w_in_t, l, cols("v"), "gelu", F32, "proj_v")
        gate_b = _proj(h, w_in_t, l, cols("gate_b"), "silu", F32, "proj_gate_b")
        y_b = _spatial(u, v, gate_b, w_s[l], b_s[l], v_norm_g[l], v_norm_b[l])
        xf = _outproj(y_a, y_b, w_out_bf, l, xf)
    return _rmsnorm(xf, final_norm_g, x.dtype).reshape(batch, seq, d)
```

```python
import functools

import jax
import jax.numpy as jnp
from jax import lax
from jax.experimental import pallas as pl
from jax.experimental.pallas import tpu as pltpu

F32, BF16, I32 = jnp.float32, jnp.bfloat16, jnp.int32

EPS = 1e-6
TOPK_MAX = 256
SQRT_HALF = 0.7071067811865476
LOG2E = 1.4426950408889634

V7X_LANES = 128
V7X_SUBLANES = 8
BF16_SUBLANES = 2 * V7X_SUBLANES
V7X_VMEM_LIMIT_BYTES = 56 * 2**20

ONES_ROWS = 16

INT_MIN = -(2**31)
MASK_NEG = -1e30


def _cparams(*sem):
    return pltpu.CompilerParams(dimension_semantics=sem, vmem_limit_bytes=V7X_VMEM_LIMIT_BYTES)


def _tile(n, want):
    t = min(n, want)
    while n % t:
        t //= 2
    return t


def _rmsnorm_kernel(x_ref, g_ref, o_ref):
    x = x_ref[...]
    ms = jnp.mean(x * x, axis=-1, keepdims=True)
    o_ref[...] = (x * lax.rsqrt(ms + EPS) * g_ref[...]).astype(o_ref.dtype)


def _rmsnorm(x, g, out_dtype):
    m, d = x.shape
    tm = _tile(m, 512)
    return pl.pallas_call(
        _rmsnorm_kernel,
        grid=(m // tm,),
        in_specs=[pl.BlockSpec((tm, d), lambda i: (i, 0)), pl.BlockSpec((1, d), lambda i: (0, 0))],
        out_specs=pl.BlockSpec((tm, d), lambda i: (i, 0)),
        out_shape=jax.ShapeDtypeStruct((m, d), out_dtype),
        compiler_params=_cparams("parallel"),
        name="rmsnorm",
    )(x, g.reshape(1, d))


def _gelu(x):
    return 0.5 * x * (1.0 + lax.erf(x * SQRT_HALF))


_ACTS = {"none": lambda x: x, "silu": jax.nn.silu, "gelu": _gelu}


NT_DIMS = (((1,), (1,)), ((), ()))


def _proj_kernel(h_ref, w_ref, o_ref, *, act):
    acc = lax.dot_general(h_ref[...], w_ref[0], NT_DIMS, preferred_element_type=F32)
    o_ref[...] = _ACTS[act](acc).astype(o_ref.dtype)


def _wrows(layer, row0, rows, k, step=None):
    def index(*ids):
        r = row0 if step is None else row0 + ids[0] * step
        return (layer, pl.multiple_of(r, BF16_SUBLANES), 0)

    assert row0 % BF16_SUBLANES == 0 and (step is None or step % BF16_SUBLANES == 0)
    return pl.BlockSpec((pl.Element(1), pl.Element(rows), pl.Element(k)), index)


def _proj(h, wt, layer, cols, act, out_dtype, name):
    m, k = h.shape
    c0, n = cols
    tm, tn = _tile(m, 1024), _tile(n, 1024)
    return pl.pallas_call(
        functools.partial(_proj_kernel, act=act),
        grid=(n // tn, m // tm),
        in_specs=[pl.BlockSpec((tm, k), lambda j, i: (i, 0)), _wrows(layer, c0, tn, k, step=tn)],
        out_specs=pl.BlockSpec((tm, tn), lambda j, i: (i, j)),
        out_shape=jax.ShapeDtypeStruct((m, n), out_dtype),
        compiler_params=_cparams("parallel", "parallel"),
        name=name,
    )(h, wt)


def _latent_kernel(h_ref, w_ref, g_ref, wuk_ref, wuvt_ref, k_ref, vt_ref, *, kscale):
    c = lax.dot_general(h_ref[...], w_ref[0], NT_DIMS, preferred_element_type=F32)
    ms = jnp.mean(c * c, axis=-1, keepdims=True)
    c = (c * lax.rsqrt(ms + EPS) * g_ref[...]).astype(BF16)
    k_ref[...] = (jnp.dot(c, wuk_ref[...], preferred_element_type=F32) * kscale).astype(k_ref.dtype)
    heads, dh, kb = vt_ref.shape[1], vt_ref.shape[2] - ONES_ROWS, c.shape[0]
    vt = lax.dot_general(wuvt_ref[...], c, NT_DIMS, preferred_element_type=F32).astype(vt_ref.dtype)
    ones_row = (lax.broadcasted_iota(I32, (ONES_ROWS, kb), 0) == 0).astype(vt_ref.dtype)
    for hd in range(heads):
        vt_ref[0, hd, :dh, :] = vt[hd * dh:(hd + 1) * dh, :]
        vt_ref[0, hd, dh:, :] = ones_row


def _latent_keys_values(h, wt, layer, cols, g, wuk, wuvt, kscale, kb):
    m, k = h.shape
    c0, cdim = cols
    heads, dh, _ = wuvt.shape
    assert m % kb == 0
    return pl.pallas_call(
        functools.partial(_latent_kernel, kscale=kscale),
        grid=(m // kb,),
        in_specs=[pl.BlockSpec((kb, k), lambda i: (i, 0)), _wrows(layer, c0, cdim, k),
                  pl.BlockSpec((1, cdim), lambda i: (0, 0)), pl.BlockSpec((cdim, heads * dh), lambda i: (0, 0)),
                  pl.BlockSpec((heads * dh, cdim), lambda i: (0, 0))],
        out_specs=[pl.BlockSpec((kb, heads * dh), lambda i: (i, 0)),
                   pl.BlockSpec((1, heads, dh + ONES_ROWS, kb), lambda i: (i, 0, 0, 0))],
        out_shape=[jax.ShapeDtypeStruct((m, heads * dh), BF16),
                   jax.ShapeDtypeStruct((m // kb, heads, dh + ONES_ROWS, kb), BF16)],
        compiler_params=_cparams("parallel"),
        name="latent_keys_values",
    )(h, wt, g.reshape(1, cdim), wuk, wuvt.reshape(heads * dh, cdim))


def _idx_kernel(h_ref, w_ref, g_ref, b_ref, k_ref, wt_ref, *, wscale):
    di, hi = k_ref.shape[1], wt_ref.shape[0]
    r = lax.dot_general(h_ref[...], w_ref[0], NT_DIMS, preferred_element_type=F32)
    k = r[:, :di]
    mu = jnp.mean(k, axis=-1, keepdims=True)
    kc = k - mu
    var = jnp.mean(kc * kc, axis=-1, keepdims=True)
    k_ref[...] = (kc * lax.rsqrt(var + EPS) * g_ref[...] + b_ref[...]).astype(k_ref.dtype)
    wt_ref[...] = r[:, di:].T[:hi, :] * wscale


def _idx_small(h, wt, layer, kcols, wcols, g, b, wscale):
    m, k = h.shape
    (k0, di), (w0, hi) = kcols, wcols
    assert w0 == k0 + di and hi <= di
    tm = _tile(m, 512)
    return pl.pallas_call(
        functools.partial(_idx_kernel, wscale=wscale),
        grid=(m // tm,),
        in_specs=[pl.BlockSpec((tm, k), lambda i: (i, 0)), _wrows(layer, k0, 2 * di, k),
                  pl.BlockSpec((1, di), lambda i: (0, 0)),
                  pl.BlockSpec((1, di), lambda i: (0, 0))],
        out_specs=[pl.BlockSpec((tm, di), lambda i: (i, 0)), pl.BlockSpec((hi, tm), lambda i: (0, i))],
        out_shape=[jax.ShapeDtypeStruct((m, di), BF16), jax.ShapeDtypeStruct((hi, m), F32)],
        compiler_params=_cparams("parallel"),
        name="idx_small",
    )(h, wt, g.reshape(1, di), b.reshape(1, di))


def _spatial_kernel(u_ref, v_ref, gb_ref, ws_ref, bst_ref, vg_ref, vb_ref, o_ref, *, chunk, groups):
    tm, bw = v_ref.shape
    e = bw // groups
    v = v_ref[...]
    mu = jnp.mean(v, axis=-1, keepdims=True)
    vc = v - mu
    var = jnp.mean(vc * vc, axis=-1, keepdims=True)
    vn = (vc * lax.rsqrt(var + EPS) * vg_ref[...] + vb_ref[...]).astype(BF16)
    row = lax.broadcasted_iota(I32, (chunk, chunk), 0)
    col = lax.broadcasted_iota(I32, (chunk, chunk), 1)
    causal = col <= row
    for g in range(groups):
        wsg = jnp.where(causal, ws_ref[g], 0.0).astype(BF16)
        bias = jnp.broadcast_to(bst_ref[:, g:g + 1], (chunk, e))
        for c in range(tm // chunk):
            rows, cols = slice(c * chunk, (c + 1) * chunk), slice(g * e, (g + 1) * e)
            mixed = jnp.dot(wsg, vn[rows, cols], preferred_element_type=F32) + bias
            o_ref[rows, cols] = (u_ref[rows, cols] * mixed * gb_ref[rows, cols]).astype(o_ref.dtype)


def _spatial(u, v, gate, w_s, b_s, vg, vb):
    m, bw = u.shape
    groups, chunk, _ = w_s.shape
    tm = 2 * chunk if m % (2 * chunk) == 0 else chunk
    row = lambda i: (i, 0)
    return pl.pallas_call(
        functools.partial(_spatial_kernel, chunk=chunk, groups=groups),
        grid=(m // tm,),
        in_specs=[pl.BlockSpec((tm, bw), row), pl.BlockSpec((tm, bw), row), pl.BlockSpec((tm, bw), row),
                  pl.BlockSpec((groups, chunk, chunk), lambda i: (0, 0, 0)),
                  pl.BlockSpec((chunk, groups), lambda i: (0, 0)),
                  pl.BlockSpec((1, bw), lambda i: (0, 0)), pl.BlockSpec((1, bw), lambda i: (0, 0))],
        out_specs=pl.BlockSpec((tm, bw), row),
        out_shape=jax.ShapeDtypeStruct((m, bw), BF16),
        compiler_params=_cparams("parallel"),
        name="spatial_gating",
    )(u, v, gate, w_s, b_s.T, vg.reshape(1, bw), vb.reshape(1, bw))


KB_PER_STEP = 2
SELECT_CHUNK_STEP = 8
WORD_BITS = 32
CHUNK_ROWS = WORD_BITS * V7X_SUBLANES


def _bit_transpose32(a):
    a = list(a)
    m, j = 0x0000FFFF, 16
    while j:
        k = 0
        while k < WORD_BITS:
            t = (a[k] ^ lax.shift_right_logical(a[k + j], jnp.int32(j))) & jnp.int32(m)
            a[k] = a[k] ^ t
            a[k + j] = a[k + j] ^ lax.shift_left(t, jnp.int32(j))
            k = (k + j + 1) & ~j
        j >>= 1
        m = (m ^ (m << j)) & 0xFFFFFFFF
    return a


def _attn_kernel(qi_ref, ki_ref, q_ref, qidx_ref, wt_ref, gate_ref, kidx_ref, kp_ref, vt_ref, o_ref,
                 planes_ref, alive_ref, sel_ref, bias_ref, acc_ref, m_ref,
                 *, qb, kb, topk, seq_bits):
    t = pl.program_id(1)
    i, kg = qi_ref[t], ki_ref[t]
    heads, dh = vt_ref.shape[1], vt_ref.shape[2] - ONES_ROWS
    iheads, di = wt_ref.shape[0], kidx_ref.shape[1]
    cpk = kb // CHUNK_ROWS
    n_kb = ((i + 1) * qb + kb - 1) // kb
    nt = NT_DIMS

    @pl.when(kg == 0)
    def _select():
        wt = wt_ref[...]
        qpos = i * qb + lax.broadcasted_iota(I32, (1, qb), 1)

        @pl.when(i == 0)
        def _():
            planes_ref[...] = jnp.zeros(planes_ref.shape, I32)

        def score_block(kbj, masked):
            r0 = pl.multiple_of(kbj * kb, kb)
            kblk = kidx_ref[pl.ds(r0, kb), :]
            score = jnp.zeros((kb, qb), F32)
            for h in range(iheads):
                lt = lax.dot_general(kblk, qidx_ref[:, h * di:(h + 1) * di], nt, preferred_element_type=F32)
                score = score + jnp.maximum(lt, 0.0) * wt[h:h + 1, :]
            bits = pltpu.bitcast(score, I32)
            key = bits ^ ((bits >> 31) | INT_MIN)
            key = jnp.where(key == 0x7FFFFFFF, INT_MIN, key)
            if masked:
                kpos = r0 + lax.broadcasted_iota(I32, (kb, 1), 0)
                key = jnp.where(kpos <= qpos, key, 0)
            for cc in range(cpk):
                rows = [key[cc * CHUNK_ROWS + j * V7X_SUBLANES:cc * CHUNK_ROWS + (j + 1) * V7X_SUBLANES, :]
                        for j in range(WORD_BITS)]
                planes = _bit_transpose32(rows)
                for b in range(WORD_BITS):
                    planes_ref[b, kbj * cpk + cc] = planes[b]

        def score_body(kbj, carry):
            score_block(kbj, False)
            return carry

        lax.fori_loop(0, n_kb - 1, score_body, 0)
        score_block(n_kb - 1, True)

        def popsum(words):
            return jnp.sum(jnp.sum(lax.population_count(words), axis=0), axis=0, keepdims=True)

        n_used = n_kb * cpk

        def select_topk(nc):
            shape = (nc,) + alive_ref.shape[1:]
            cidx = lax.broadcasted_iota(I32, shape, 0)
            alive_ref[:nc] = jnp.where(cidx < n_used, -1, 0)
            sel_ref[:nc] = jnp.zeros(shape, I32)

            def radix_body(b, k_rem):
                plane = planes_ref[b, :nc]
                alive = alive_ref[:nc]
                ones = alive & plane
                cnt = popsum(ones)
                take1 = cnt >= k_rem
                alive_ref[:nc] = jnp.where(take1, ones, alive & ~plane)
                sel_ref[:nc] = jnp.where(take1, sel_ref[:nc], sel_ref[:nc] | ones)
                return jnp.where(take1, k_rem, k_rem - cnt)

            need = lax.fori_loop(0, WORD_BITS, radix_body, jnp.minimum(qpos + 1, topk))

            @pl.when(jnp.max(popsum(alive_ref[:nc]) - need) > 0)
            def _():
                ties = alive_ref[:nc]
                row0 = cidx * CHUNK_ROWS + lax.broadcasted_iota(I32, shape, 1)

                def rows_below(lim):
                    nj = jnp.clip(lax.shift_right_arithmetic(lim - row0 + (V7X_SUBLANES - 1), 3), 0, WORD_BITS)
                    return jnp.where(nj <= 0, 0, lax.shift_left(jnp.int32(-1), WORD_BITS - jnp.maximum(nj, 1)))

                def idx_body(b, lim):
                    cand = lim + lax.shift_left(jnp.int32(1), seq_bits - 1 - b)
                    return jnp.where(popsum(ties & rows_below(cand)) < need, cand, lim)

                lim = lax.fori_loop(0, seq_bits, idx_body, jnp.zeros((1, qb), I32))
                alive_ref[:nc] = ties & rows_below(lim + 1)

            sel_ref[:nc] = sel_ref[:nc] | alive_ref[:nc]

        nc_total = alive_ref.shape[0]
        sizes = sorted({min(nc_total, s) for s in range(SELECT_CHUNK_STEP, nc_total + SELECT_CHUNK_STEP, SELECT_CHUNK_STEP)})
        for lo, nc in zip([0] + sizes[:-1], sizes):
            pl.when((n_used > lo) & (n_used <= nc))(functools.partial(select_topk, nc))

        def bias_body(c, carry):
            words = sel_ref[c]
            rows8 = []
            for j in range(WORD_BITS):
                top = words if j == 0 else lax.shift_left(words, jnp.int32(j))
                rows8.append(jnp.where(top < 0, 0.0, MASK_NEG))
            for j in range(0, WORD_BITS, 2):
                r = pl.multiple_of(c * CHUNK_ROWS, CHUNK_ROWS) + j * V7X_SUBLANES
                bias_ref[pl.ds(r, BF16_SUBLANES), :] = jnp.concatenate(rows8[j:j + 2], axis=0).astype(bias_ref.dtype)
            return carry

        lax.fori_loop(0, n_kb * cpk, bias_body, 0)
        m_ref[...] = jnp.full(m_ref.shape, MASK_NEG, F32)
        acc_ref[...] = jnp.zeros(acc_ref.shape, F32)

    def block_body(sub, carry):
        r0 = pl.multiple_of((kg * KB_PER_STEP + sub) * kb, kb)
        rk = pl.multiple_of(sub * kb, kb)
        bias = bias_ref[pl.ds(r0, kb), :].astype(F32)
        s = jnp.concatenate(
            [lax.dot_general(kp_ref[pl.ds(rk, kb), h * dh:(h + 1) * dh], q_ref[:, h * dh:(h + 1) * dh], nt,
                             preferred_element_type=F32) + bias for h in range(heads)], axis=1)
        m_old = m_ref[...]
        m_new = jnp.maximum(m_old, jnp.max(s, axis=0, keepdims=True))
        alpha = jnp.exp2(m_old - m_new)
        p = jnp.exp2(s - m_new).astype(vt_ref.dtype)
        m_ref[...] = m_new
        for h in range(heads):
            qc = slice(h * qb, (h + 1) * qb)
            pv = jnp.dot(vt_ref[sub, h], p[:, qc], preferred_element_type=F32)
            acc_ref[h] = acc_ref[h] * alpha[:, qc] + pv
        return carry

    lax.fori_loop(0, jnp.minimum(KB_PER_STEP, n_kb - kg * KB_PER_STEP), block_body, 0)

    @pl.when((kg + 1) * KB_PER_STEP >= n_kb)
    def _finish():
        for h in range(heads):
            hc = slice(h * dh, (h + 1) * dh)
            o_t = acc_ref[h, :dh, :] * (1.0 / acc_ref[h, dh:dh + 1, :])
            o_ref[:, hc] = (o_t.T * gate_ref[:, hc]).astype(o_ref.dtype)


def _causal_steps(nq, qb, kb):
    qi, ki = [], []
    for i in range(nq):
        n_kb = ((i + 1) * qb + kb - 1) // kb
        for k in range((n_kb + KB_PER_STEP - 1) // KB_PER_STEP):
            qi.append(i)
            ki.append(k)
    return jnp.asarray(qi, I32), jnp.asarray(ki, I32)


def _sparse_attention(q, qidx, wt, gate, kidx, kproj, vt, *, batch, kb):
    m, aw = q.shape
    seq = m // batch
    heads, dh = vt.shape[1], vt.shape[2] - ONES_ROWS
    iheads, di = wt.shape[0], kidx.shape[1]
    qb = min(2 * V7X_LANES, seq)
    nq, nkg = seq // qb, seq // (KB_PER_STEP * kb)
    assert seq % (KB_PER_STEP * kb) == 0 and kb % CHUNK_ROWS == 0
    topk = min(TOPK_MAX, seq // 4)
    qi, ki = _causal_steps(nq, qb, kb)
    qrow = lambda b, t, qi, ki: (b * nq + qi[t], 0)
    nchunks = seq // CHUNK_ROWS
    grid_spec = pltpu.PrefetchScalarGridSpec(
        num_scalar_prefetch=2,
        grid=(batch, qi.shape[0]),
        in_specs=[pl.BlockSpec((qb, aw), qrow),
                  pl.BlockSpec((qb, iheads * di), qrow),
                  pl.BlockSpec((iheads, qb), lambda b, t, qi, ki: (0, b * nq + qi[t])),
                  pl.BlockSpec((qb, aw), qrow),
                  pl.BlockSpec((seq, di), lambda b, t, qi, ki: (b, 0), pipeline_mode=pl.Buffered(1)),
                  pl.BlockSpec((KB_PER_STEP * kb, aw), lambda b, t, qi, ki: (b * nkg + ki[t], 0)),
                  pl.BlockSpec((KB_PER_STEP, heads, dh + ONES_ROWS, kb), lambda b, t, qi, ki: (b * nkg + ki[t], 0, 0, 0))],
        out_specs=pl.BlockSpec((qb, aw), qrow),
        scratch_shapes=[pltpu.VMEM((WORD_BITS, nchunks, V7X_SUBLANES, qb), I32),
                        pltpu.VMEM((nchunks, V7X_SUBLANES, qb), I32),
                        pltpu.VMEM((nchunks, V7X_SUBLANES, qb), I32),
                        pltpu.VMEM((seq, qb), BF16),
                        pltpu.VMEM((heads, dh + ONES_ROWS, qb), F32),
                        pltpu.VMEM((1, heads * qb), F32)])
    return pl.pallas_call(
        functools.partial(_attn_kernel, qb=qb, kb=kb, topk=topk, seq_bits=seq.bit_length()),
        grid_spec=grid_spec,
        out_shape=jax.ShapeDtypeStruct((m, aw), BF16),
        compiler_params=_cparams("parallel", "arbitrary"),
        name="sparse_attention",
    )(qi, ki, q, qidx, wt, gate, kidx, kproj, vt)


def _outproj_kernel(ya_ref, yb_ref, wa_ref, wb_ref, x_ref, o_ref):
    acc = jnp.dot(ya_ref[...], wa_ref[...], preferred_element_type=F32)
    acc = acc + jnp.dot(yb_ref[...], wb_ref[...], preferred_element_type=F32)
    o_ref[...] = x_ref[...] + acc


def _wspec(w, layer, rows, cols, index):
    return pl.BlockSpec((None, rows, cols), lambda *ids: (layer,) + tuple(index(*ids)))


def _outproj(ya, yb, w, layer, x):
    m, ka = ya.shape
    kbw = yb.shape[1]
    n = w.shape[2]
    assert ka == kbw and w.shape[1] == ka + kbw
    tm, tn = _tile(m, 1024), _tile(n, 1024)
    return pl.pallas_call(
        _outproj_kernel,
        grid=(n // tn, m // tm),
        in_specs=[pl.BlockSpec((tm, ka), lambda j, i: (i, 0)), pl.BlockSpec((tm, kbw), lambda j, i: (i, 0)),
                  _wspec(w, layer, ka, tn, lambda j, i: (0, j)), _wspec(w, layer, kbw, tn, lambda j, i: (1, j)),
                  pl.BlockSpec((tm, tn), lambda j, i: (i, j))],
        out_specs=pl.BlockSpec((tm, tn), lambda j, i: (i, j)),
        out_shape=jax.ShapeDtypeStruct((m, n), F32),
        compiler_params=_cparams("parallel", "parallel"),
        name="outproj",
    )(ya, yb, w, w, x)


def kernel(x, norm_g, w_in, kv_norm_g, idx_k_norm_g, idx_k_norm_b, w_uk, w_uv, v_norm_g, v_norm_b, w_s, b_s,
           w_out, final_norm_g):
    batch, seq, d = x.shape
    depth = w_in.shape[0]
    cdim, heads, dh = w_uk.shape[1:]
    aw = heads * dh
    di = idx_k_norm_g.shape[1]
    bw = v_norm_g.shape[1]
    iheads = (w_in.shape[2] - 2 * aw - cdim - di - 3 * bw) // (di + 1)
    kb = _tile(seq, 512)

    names = ("q", "c_kv", "gate_a", "q_idx", "k_idx", "w_idx", "u", "v", "gate_b")
    sizes = (aw, cdim, aw, iheads * di, di, iheads, bw, bw, bw)
    seg, off = {}, 0
    for nm, sz in zip(names, sizes):
        seg[nm] = (off, off + sz)
        off += sz

    w_in_t = jnp.swapaxes(w_in, 1, 2).astype(BF16)
    w_out_bf = w_out.astype(BF16)
    cols = lambda nm: (seg[nm][0], seg[nm][1] - seg[nm][0])

    xf = x.reshape(batch * seq, d)
    for l in range(depth):
        h = _rmsnorm(xf, norm_g[l], BF16)
        q = _proj(h, w_in_t, l, cols("q"), "none", BF16, "proj_q")
        kproj, vt = _latent_keys_values(h, w_in_t, l, cols("c_kv"), kv_norm_g[l],
                                        w_uk[l].reshape(cdim, aw).astype(BF16),
                                        jnp.transpose(w_uv[l], (1, 2, 0)).astype(BF16), (dh ** -0.5) * LOG2E, kb)
        gate_a = _proj(h, w_in_t, l, cols("gate_a"), "silu", F32, "proj_gate_a")
        qidx = _proj(h, w_in_t, l, cols("q_idx"), "none", BF16, "proj_q_idx")
        kidx, wt = _idx_small(h, w_in_t, l, cols("k_idx"), cols("w_idx"), idx_k_norm_g[l], idx_k_norm_b[l],
                              (iheads ** -0.5) * (di ** -0.5))
        y_a = _sparse_attention(q, qidx, wt, gate_a, kidx, kproj, vt, batch=batch, kb=kb)
        u = _proj(h, w_in_t, l, cols("u"), "gelu", F32, "proj_u")
        v = _proj(h, w_in_t, l, cols("v"), "gelu", F32, "proj_v")
        gate_b = _proj(h, w_in_t, l, cols("gate_b"), "silu", F32, "proj_gate_b")
        y_b = _spatial(u, v, gate_b, w_s[l], b_s[l], v_norm_g[l], v_norm_b[l])
        xf = _outproj(y_a, y_b, w_out_bf, l, xf)
    return _rmsnorm(xf, final_norm_g, x.dtype).reshape(batch, seq, d)
```

```python
import functools

import jax
import jax.numpy as jnp
from jax import lax
from jax.experimental import pallas as pl
from jax.experimental.pallas import tpu as pltpu

F32, BF16, I32 = jnp.float32, jnp.bfloat16, jnp.int32

EPS = 1e-6
TOPK_MAX = 256
SQRT_HALF = 0.7071067811865476
LOG2E = 1.4426950408889634

V7X_LANES = 128
V7X_SUBLANES = 8
BF16_SUBLANES = 2 * V7X_SUBLANES
V7X_VMEM_LIMIT_BYTES = 56 * 2**20

ONES_ROWS = 16

INT_MIN = -(2**31)
MASK_NEG = -1e30


def _cparams(*sem):
    return pltpu.CompilerParams(dimension_semantics=sem, vmem_limit_bytes=V7X_VMEM_LIMIT_BYTES)


def _tile(n, want):
    t = min(n, want)
    while n % t:
        t //= 2
    return t


def _rmsnorm_kernel(x_ref, g_ref, o_ref):
    x = x_ref[...]
    ms = jnp.mean(x * x, axis=-1, keepdims=True)
    o_ref[...] = (x * lax.rsqrt(ms + EPS) * g_ref[...]).astype(o_ref.dtype)


def _rmsnorm(x, g, out_dtype):
    m, d = x.shape
    tm = _tile(m, 512)
    return pl.pallas_call(
        _rmsnorm_kernel,
        grid=(m // tm,),
        in_specs=[pl.BlockSpec((tm, d), lambda i: (i, 0)), pl.BlockSpec((1, d), lambda i: (0, 0))],
        out_specs=pl.BlockSpec((tm, d), lambda i: (i, 0)),
        out_shape=jax.ShapeDtypeStruct((m, d), out_dtype),
        compiler_params=_cparams("parallel"),
        name="rmsnorm",
    )(x, g.reshape(1, d))


def _gelu(x):
    return 0.5 * x * (1.0 + lax.erf(x * SQRT_HALF))


_ACTS = {"none": lambda x: x, "silu": jax.nn.silu, "gelu": _gelu}


NT_DIMS = (((1,), (1,)), ((), ()))


def _proj_kernel(h_ref, w_ref, o_ref, *, act):
    acc = lax.dot_general(h_ref[...], w_ref[0], NT_DIMS, preferred_element_type=F32)
    o_ref[...] = _ACTS[act](acc).astype(o_ref.dtype)


def _wrows(layer, row0, rows, k, step=None):
    def index(*ids):
        r = row0 if step is None else row0 + ids[0] * step
        return (layer, pl.multiple_of(r, BF16_SUBLANES), 0)

    assert row0 % BF16_SUBLANES == 0 and (step is None or step % BF16_SUBLANES == 0)
    return pl.BlockSpec((pl.Element(1), pl.Element(rows), pl.Element(k)), index)


def _proj(h, wt, layer, cols, act, out_dtype, name):
    m, k = h.shape
    c0, n = cols
    tm, tn = _tile(m, 1024), _tile(n, 1024)
    return pl.pallas_call(
        functools.partial(_proj_kernel, act=act),
        grid=(n // tn, m // tm),
        in_specs=[pl.BlockSpec((tm, k), lambda j, i: (i, 0)), _wrows(layer, c0, tn, k, step=tn)],
        out_specs=pl.BlockSpec((tm, tn), lambda j, i: (i, j)),
        out_shape=jax.ShapeDtypeStruct((m, n), out_dtype),
        compiler_params=_cparams("parallel", "parallel"),
        name=name,
    )(h, wt)


def _latent_kernel(h_ref, w_ref, g_ref, wuk_ref, wuvt_ref, k_ref, vt_ref, *, kscale):
    c = lax.dot_general(h_ref[...], w_ref[0], NT_DIMS, preferred_element_type=F32)
    ms = jnp.mean(c * c, axis=-1, keepdims=True)
    c = (c * lax.rsqrt(ms + EPS) * g_ref[...]).astype(BF16)
    k_ref[...] = (jnp.dot(c, wuk_ref[...], preferred_element_type=F32) * kscale).astype(k_ref.dtype)
    heads, dh, kb = vt_ref.shape[1], vt_ref.shape[2] - ONES_ROWS, c.shape[0]
    vt = lax.dot_general(wuvt_ref[...], c, NT_DIMS, preferred_element_type=F32).astype(vt_ref.dtype)
    ones_row = (lax.broadcasted_iota(I32, (ONES_ROWS, kb), 0) == 0).astype(vt_ref.dtype)
    for hd in range(heads):
        vt_ref[0, hd, :dh, :] = vt[hd * dh:(hd + 1) * dh, :]
        vt_ref[0, hd, dh:, :] = ones_row


def _latent_keys_values(h, wt, layer, cols, g, wuk, wuvt, kscale, kb):
    m, k = h.shape
    c0, cdim = cols
    heads, dh, _ = wuvt.shape
    assert m % kb == 0
    return pl.pallas_call(
        functools.partial(_latent_kernel, kscale=kscale),
        grid=(m // kb,),
        in_specs=[pl.BlockSpec((kb, k), lambda i: (i, 0)), _wrows(layer, c0, cdim, k),
                  pl.BlockSpec((1, cdim), lambda i: (0, 0)), pl.BlockSpec((cdim, heads * dh), lambda i: (0, 0)),
                  pl.BlockSpec((heads * dh, cdim), lambda i: (0, 0))],
        out_specs=[pl.BlockSpec((kb, heads * dh), lambda i: (i, 0)),
                   pl.BlockSpec((1, heads, dh + ONES_ROWS, kb), lambda i: (i, 0, 0, 0))],
        out_shape=[jax.ShapeDtypeStruct((m, heads * dh), BF16),
                   jax.ShapeDtypeStruct((m // kb, heads, dh + ONES_ROWS, kb), BF16)],
        compiler_params=_cparams("parallel"),
        name="latent_keys_values",
    )(h, wt, g.reshape(1, cdim), wuk, wuvt.reshape(heads * dh, cdim))


def _idx_kernel(h_ref, w_ref, g_ref, b_ref, k_ref, wt_ref, *, wscale):
    di, hi = k_ref.shape[1], wt_ref.shape[0]
    r = lax.dot_general(h_ref[...], w_ref[0], NT_DIMS, preferred_element_type=F32)
    k = r[:, :di]
    mu = jnp.mean(k, axis=-1, keepdims=True)
    kc = k - mu
    var = jnp.mean(kc * kc, axis=-1, keepdims=True)
    k_ref[...] = (kc * lax.rsqrt(var + EPS) * g_ref[...] + b_ref[...]).astype(k_ref.dtype)
    wt_ref[...] = r[:, di:].T[:hi, :] * wscale


def _idx_small(h, wt, layer, kcols, wcols, g, b, wscale):
    m, k = h.shape
    (k0, di), (w0, hi) = kcols, wcols
    assert w0 == k0 + di and hi <= di
    tm = _tile(m, 512)
    return pl.pallas_call(
        functools.partial(_idx_kernel, wscale=wscale),
        grid=(m // tm,),
        in_specs=[pl.BlockSpec((tm, k), lambda i: (i, 0)), _wrows(layer, k0, 2 * di, k),
                  pl.BlockSpec((1, di), lambda i: (0, 0)),
                  pl.BlockSpec((1, di), lambda i: (0, 0))],
        out_specs=[pl.BlockSpec((tm, di), lambda i: (i, 0)), pl.BlockSpec((hi, tm), lambda i: (0, i))],
        out_shape=[jax.ShapeDtypeStruct((m, di), BF16), jax.ShapeDtypeStruct((hi, m), F32)],
        compiler_params=_cparams("parallel"),
        name="idx_small",
    )(h, wt, g.reshape(1, di), b.reshape(1, di))


def _spatial_kernel(u_ref, v_ref, gb_ref, ws_ref, bst_ref, vg_ref, vb_ref, o_ref, *, chunk, groups):
    tm, bw = v_ref.shape
    e = bw // groups
    v = v_ref[...]
    mu = jnp.mean(v, axis=-1, keepdims=True)
    vc = v - mu
    var = jnp.mean(vc * vc, axis=-1, keepdims=True)
    vn = (vc * lax.rsqrt(var + EPS) * vg_ref[...] + vb_ref[...]).astype(BF16)
    row = lax.broadcasted_iota(I32, (chunk, chunk), 0)
    col = lax.broadcasted_iota(I32, (chunk, chunk), 1)
    causal = col <= row
    for g in range(groups):
        wsg = jnp.where(causal, ws_ref[g], 0.0).astype(BF16)
        bias = jnp.broadcast_to(bst_ref[:, g:g + 1], (chunk, e))
        for c in range(tm // chunk):
            rows, cols = slice(c * chunk, (c + 1) * chunk), slice(g * e, (g + 1) * e)
            mixed = jnp.dot(wsg, vn[rows, cols], preferred_element_type=F32) + bias
            o_ref[rows, cols] = (u_ref[rows, cols] * mixed * gb_ref[rows, cols]).astype(o_ref.dtype)


def _spatial(u, v, gate, w_s, b_s, vg, vb):
    m, bw = u.shape
    groups, chunk, _ = w_s.shape
    tm = 2 * chunk if m % (2 * chunk) == 0 else chunk
    row = lambda i: (i, 0)
    return pl.pallas_call(
        functools.partial(_spatial_kernel, chunk=chunk, groups=groups),
        grid=(m // tm,),
        in_specs=[pl.BlockSpec((tm, bw), row), pl.BlockSpec((tm, bw), row), pl.BlockSpec((tm, bw), row),
                  pl.BlockSpec((groups, chunk, chunk), lambda i: (0, 0, 0)),
                  pl.BlockSpec((chunk, groups), lambda i: (0, 0)),
                  pl.BlockSpec((1, bw), lambda i: (0, 0)), pl.BlockSpec((1, bw), lambda i: (0, 0))],
        out_specs=pl.BlockSpec((tm, bw), row),
        out_shape=jax.ShapeDtypeStruct((m, bw), BF16),
        compiler_params=_cparams("parallel"),
        name="spatial_gating",
    )(u, v, gate, w_s, b_s.T, vg.reshape(1, bw), vb.reshape(1, bw))


KB_PER_STEP = 2
SELECT_CHUNK_STEP = 8
WORD_BITS = 32
CHUNK_ROWS = WORD_BITS * V7X_SUBLANES


def _bit_transpose32(a):
    a = list(a)
    m, j = 0x0000FFFF, 16
    while j:
        k = 0
        while k < WORD_BITS:
            t = (a[k] ^ lax.shift_right_logical(a[k + j], jnp.int32(j))) & jnp.int32(m)
            a[k] = a[k] ^ t
            a[k + j] = a[k + j] ^ lax.shift_left(t, jnp.int32(j))
            k = (k + j + 1) & ~j
        j >>= 1
        m = (m ^ (m << j)) & 0xFFFFFFFF
    return a


def _attn_kernel(qi_ref, ki_ref, q_ref, qidx_ref, wt_ref, gate_ref, kidx_ref, kp_ref, vt_ref, o_ref,
                 planes_ref, alive_ref, sel_ref, bias_ref, acc_ref, m_ref,
                 *, qb, kb, topk, seq_bits):
    t = pl.program_id(1)
    i, kg = qi_ref[t], ki_ref[t]
    heads, dh = vt_ref.shape[1], vt_ref.shape[2] - ONES_ROWS
    iheads, di = wt_ref.shape[0], kidx_ref.shape[1]
    cpk = kb // CHUNK_ROWS
    n_kb = ((i + 1) * qb + kb - 1) // kb
    half_tail = n_kb * kb - (i + 1) * qb >= kb // 2
    nt = NT_DIMS

    @pl.when(kg == 0)
    def _select():
        wt = wt_ref[...]
        qpos = i * qb + lax.broadcasted_iota(I32, (1, qb), 1)

        @pl.when(i == 0)
        def _():
            planes_ref[...] = jnp.zeros(planes_ref.shape, I32)

        def score_block(kbj, masked, rows=kb):
            r0 = pl.multiple_of(kbj * kb, kb)
            kblk = kidx_ref[pl.ds(r0, rows), :]
            score = jnp.zeros((rows, qb), F32)
            for h in range(iheads):
                lt = lax.dot_general(kblk, qidx_ref[:, h * di:(h + 1) * di], nt, preferred_element_type=F32)
                score = score + jnp.maximum(lt, 0.0) * wt[h:h + 1, :]
            bits = pltpu.bitcast(score, I32)
            key = bits ^ ((bits >> 31) | INT_MIN)
            key = jnp.where(key == 0x7FFFFFFF, INT_MIN, key)
            if masked:
                kpos = r0 + lax.broadcasted_iota(I32, (rows, 1), 0)
                key = jnp.where(kpos <= qpos, key, 0)
            for cc in range(rows // CHUNK_ROWS):
                rws = [key[cc * CHUNK_ROWS + j * V7X_SUBLANES:cc * CHUNK_ROWS + (j + 1) * V7X_SUBLANES, :]
                       for j in range(WORD_BITS)]
                planes = _bit_transpose32(rws)
                for b in range(WORD_BITS):
                    planes_ref[b, kbj * cpk + cc] = planes[b]

        def score_body(kbj, carry):
            score_block(kbj, False)
            return carry

        lax.fori_loop(0, n_kb - 1, score_body, 0)
        pl.when(half_tail)(functools.partial(score_block, n_kb - 1, True, kb // 2))
        pl.when(jnp.logical_not(half_tail))(functools.partial(score_block, n_kb - 1, True))

        def popsum(words):
            return jnp.sum(jnp.sum(lax.population_count(words), axis=0), axis=0, keepdims=True)

        n_used = ((i + 1) * qb + CHUNK_ROWS - 1) // CHUNK_ROWS

        def select_topk(nc):
            shape = (nc,) + alive_ref.shape[1:]
            cidx = lax.broadcasted_iota(I32, shape, 0)
            alive_ref[:nc] = jnp.where(cidx < n_used, -1, 0)
            sel_ref[:nc] = jnp.zeros(shape, I32)

            def radix_body(b, k_rem):
                plane = planes_ref[b, :nc]
                alive = alive_ref[:nc]
                ones = alive & plane
                cnt = popsum(ones)
                take1 = cnt >= k_rem
                alive_ref[:nc] = jnp.where(take1, ones, alive & ~plane)
                sel_ref[:nc] = jnp.where(take1, sel_ref[:nc], sel_ref[:nc] | ones)
                return jnp.where(take1, k_rem, k_rem - cnt)

            need = lax.fori_loop(0, WORD_BITS, radix_body, jnp.minimum(qpos + 1, topk))

            @pl.when(jnp.max(popsum(alive_ref[:nc]) - need) > 0)
            def _():
                ties = alive_ref[:nc]
                row0 = cidx * CHUNK_ROWS + lax.broadcasted_iota(I32, shape, 1)

                def rows_below(lim):
                    nj = jnp.clip(lax.shift_right_arithmetic(lim - row0 + (V7X_SUBLANES - 1), 3), 0, WORD_BITS)
                    return jnp.where(nj <= 0, 0, lax.shift_left(jnp.int32(-1), WORD_BITS - jnp.maximum(nj, 1)))

                def idx_body(b, lim):
                    cand = lim + lax.shift_left(jnp.int32(1), seq_bits - 1 - b)
                    return jnp.where(popsum(ties & rows_below(cand)) < need, cand, lim)

                lim = lax.fori_loop(0, seq_bits, idx_body, jnp.zeros((1, qb), I32))
                alive_ref[:nc] = ties & rows_below(lim + 1)

            sel_ref[:nc] = sel_ref[:nc] | alive_ref[:nc]

        nc_total = alive_ref.shape[0]
        sizes = sorted({min(nc_total, s) for s in range(SELECT_CHUNK_STEP, nc_total + SELECT_CHUNK_STEP, SELECT_CHUNK_STEP)})
        for lo, nc in zip([0] + sizes[:-1], sizes):
            pl.when((n_used > lo) & (n_used <= nc))(functools.partial(select_topk, nc))

        def bias_body(c, carry):
            words = sel_ref[c]
            rows8 = []
            for j in range(WORD_BITS):
                top = words if j == 0 else lax.shift_left(words, jnp.int32(j))
                rows8.append(jnp.where(top < 0, 0.0, MASK_NEG))
            for j in range(0, WORD_BITS, 2):
                r = pl.multiple_of(c * CHUNK_ROWS, CHUNK_ROWS) + j * V7X_SUBLANES
                bias_ref[pl.ds(r, BF16_SUBLANES), :] = jnp.concatenate(rows8[j:j + 2], axis=0).astype(bias_ref.dtype)
            return carry

        lax.fori_loop(0, n_used, bias_body, 0)
        m_ref[...] = jnp.full(m_ref.shape, MASK_NEG, F32)
        acc_ref[...] = jnp.zeros(acc_ref.shape, F32)

    def attend(sub, rows):
        r0 = pl.multiple_of((kg * KB_PER_STEP + sub) * kb, kb)
        rk = pl.multiple_of(sub * kb, kb)
        bias = bias_ref[pl.ds(r0, rows), :].astype(F32)
        s = jnp.concatenate(
            [lax.dot_general(kp_ref[pl.ds(rk, rows), h * dh:(h + 1) * dh], q_ref[:, h * dh:(h + 1) * dh], nt,
                             preferred_element_type=F32) + bias for h in range(heads)], axis=1)
        m_old = m_ref[...]
        m_new = jnp.maximum(m_old, jnp.max(s, axis=0, keepdims=True))
        alpha = jnp.exp2(m_old - m_new)
        p = jnp.exp2(s - m_new).astype(vt_ref.dtype)
        m_ref[...] = m_new
        for h in range(heads):
            qc = slice(h * qb, (h + 1) * qb)
            pv = jnp.dot(vt_ref[sub, h, :, :rows], p[:, qc], preferred_element_type=F32)
            acc_ref[h] = acc_ref[h] * alpha[:, qc] + pv

    def block_body(sub, carry):
        attend(sub, kb)
        return carry

    n_here = jnp.minimum(KB_PER_STEP, n_kb - kg * KB_PER_STEP)
    is_last = (kg + 1) * KB_PER_STEP >= n_kb
    trim = jnp.logical_and(is_last, half_tail)
    lax.fori_loop(0, n_here - trim.astype(I32), block_body, 0)
    pl.when(trim)(lambda: attend(n_here - 1, kb // 2))

    @pl.when((kg + 1) * KB_PER_STEP >= n_kb)
    def _finish():
        for h in range(heads):
            hc = slice(h * dh, (h + 1) * dh)
            o_t = acc_ref[h, :dh, :] * (1.0 / acc_ref[h, dh:dh + 1, :])
            o_ref[:, hc] = (o_t.T * gate_ref[:, hc]).astype(o_ref.dtype)


def _causal_steps(nq, qb, kb):
    qi, ki = [], []
    for i in range(nq):
        n_kb = ((i + 1) * qb + kb - 1) // kb
        for k in range((n_kb + KB_PER_STEP - 1) // KB_PER_STEP):
            qi.append(i)
            ki.append(k)
    return jnp.asarray(qi, I32), jnp.asarray(ki, I32)


def _sparse_attention(q, qidx, wt, gate, kidx, kproj, vt, *, batch, kb):
    m, aw = q.shape
    seq = m // batch
    heads, dh = vt.shape[1], vt.shape[2] - ONES_ROWS
    iheads, di = wt.shape[0], kidx.shape[1]
    qb = min(2 * V7X_LANES, seq)
    nq, nkg = seq // qb, seq // (KB_PER_STEP * kb)
    assert seq % (KB_PER_STEP * kb) == 0 and kb % CHUNK_ROWS == 0
    topk = min(TOPK_MAX, seq // 4)
    qi, ki = _causal_steps(nq, qb, kb)
    qrow = lambda b, t, qi, ki: (b * nq + qi[t], 0)
    nchunks = seq // CHUNK_ROWS
    grid_spec = pltpu.PrefetchScalarGridSpec(
        num_scalar_prefetch=2,
        grid=(batch, qi.shape[0]),
        in_specs=[pl.BlockSpec((qb, aw), qrow),
                  pl.BlockSpec((qb, iheads * di), qrow),
                  pl.BlockSpec((iheads, qb), lambda b, t, qi, ki: (0, b * nq + qi[t])),
                  pl.BlockSpec((qb, aw), qrow),
                  pl.BlockSpec((seq, di), lambda b, t, qi, ki: (b, 0), pipeline_mode=pl.Buffered(1)),
                  pl.BlockSpec((KB_PER_STEP * kb, aw), lambda b, t, qi, ki: (b * nkg + ki[t], 0)),
                  pl.BlockSpec((KB_PER_STEP, heads, dh + ONES_ROWS, kb), lambda b, t, qi, ki: (b * nkg + ki[t], 0, 0, 0))],
        out_specs=pl.BlockSpec((qb, aw), qrow),
        scratch_shapes=[pltpu.VMEM((WORD_BITS, nchunks, V7X_SUBLANES, qb), I32),
                        pltpu.VMEM((nchunks, V7X_SUBLANES, qb), I32),
                        pltpu.VMEM((nchunks, V7X_SUBLANES, qb), I32),
                        pltpu.VMEM((seq, qb), BF16),
                        pltpu.VMEM((heads, dh + ONES_ROWS, qb), F32),
                        pltpu.VMEM((1, heads * qb), F32)])
    return pl.pallas_call(
        functools.partial(_attn_kernel, qb=qb, kb=kb, topk=topk, seq_bits=seq.bit_length()),
        grid_spec=grid_spec,
        out_shape=jax.ShapeDtypeStruct((m, aw), BF16),
        compiler_params=_cparams("parallel", "arbitrary"),
        name="sparse_attention",
    )(qi, ki, q, qidx, wt, gate, kidx, kproj, vt)


def _outproj_kernel(ya_ref, yb_ref, wa_ref, wb_ref, x_ref, o_ref):
    acc = jnp.dot(ya_ref[...], wa_ref[...], preferred_element_type=F32)
    acc = acc + jnp.dot(yb_ref[...], wb_ref[...], preferred_element_type=F32)
    o_ref[...] = x_ref[...] + acc


def _wspec(w, layer, rows, cols, index):
    return pl.BlockSpec((None, rows, cols), lambda *ids: (layer,) + tuple(index(*ids)))


def _outproj(ya, yb, w, layer, x):
    m, ka = ya.shape
    kbw = yb.shape[1]
    n = w.shape[2]
    assert ka == kbw and w.shape[1] == ka + kbw
    tm, tn = _tile(m, 1024), _tile(n, 1024)
    return pl.pallas_call(
        _outproj_kernel,
        grid=(n // tn, m // tm),
        in_specs=[pl.BlockSpec((tm, ka), lambda j, i: (i, 0)), pl.BlockSpec((tm, kbw), lambda j, i: (i, 0)),
                  _wspec(w, layer, ka, tn, lambda j, i: (0, j)), _wspec(w, layer, kbw, tn, lambda j, i: (1, j)),
                  pl.BlockSpec((tm, tn), lambda j, i: (i, j))],
        out_specs=pl.BlockSpec((tm, tn), lambda j, i: (i, j)),
        out_shape=jax.ShapeDtypeStruct((m, n), F32),
        compiler_params=_cparams("parallel", "parallel"),
        name="outproj",
    )(ya, yb, w, w, x)


def kernel(x, norm_g, w_in, kv_norm_g, idx_k_norm_g, idx_k_norm_b, w_uk, w_uv, v_norm_g, v_norm_b, w_s, b_s,
           w_out, final_norm_g):
    batch, seq, d = x.shape
    depth = w_in.shape[0]
    cdim, heads, dh = w_uk.shape[1:]
    aw = heads * dh
    di = idx_k_norm_g.shape[1]
    bw = v_norm_g.shape[1]
    iheads = (w_in.shape[2] - 2 * aw - cdim - di - 3 * bw) // (di + 1)
    kb = _tile(seq, 512)

    names = ("q", "c_kv", "gate_a", "q_idx", "k_idx", "w_idx", "u", "v", "gate_b")
    sizes = (aw, cdim, aw, iheads * di, di, iheads, bw, bw, bw)
    seg, off = {}, 0
    for nm, sz in zip(names, sizes):
        seg[nm] = (off, off + sz)
        off += sz

    w_in_t = jnp.swapaxes(w_in, 1, 2).astype(BF16)
    w_out_bf = w_out.astype(BF16)
    cols = lambda nm: (seg[nm][0], seg[nm][1] - seg[nm][0])

    xf = x.reshape(batch * seq, d)
    for l in range(depth):
        h = _rmsnorm(xf, norm_g[l], BF16)
        q = _proj(h, w_in_t, l, cols("q"), "none", BF16, "proj_q")
        kproj, vt = _latent_keys_values(h, w_in_t, l, cols("c_kv"), kv_norm_g[l],
                                        w_uk[l].reshape(cdim, aw).astype(BF16),
                                        jnp.transpose(w_uv[l], (1, 2, 0)).astype(BF16), (dh ** -0.5) * LOG2E, kb)
        gate_a = _proj(h, w_in_t, l, cols("gate_a"), "silu", F32, "proj_gate_a")
        qidx = _proj(h, w_in_t, l, cols("q_idx"), "none", BF16, "proj_q_idx")
        kidx, wt = _idx_small(h, w_in_t, l, cols("k_idx"), cols("w_idx"), idx_k_norm_g[l], idx_k_norm_b[l],
                              (iheads ** -0.5) * (di ** -0.5))
        y_a = _sparse_attention(q, qidx, wt, gate_a, kidx, kproj, vt, batch=batch, kb=kb)
        u = _proj(h, w_in_t, l, cols("u"), "gelu", F32, "proj_u")
        v = _proj(h, w_in_t, l, cols("v"), "gelu", F32, "proj_v")
        gate_b = _proj(h, w_in_t, l, cols("gate_b"), "silu", F32, "proj_gate_b")
        y_b = _spatial(u, v, gate_b, w_s[l], b_s[l], v_norm_g[l], v_norm_b[l])
        xf = _outproj(y_a, y_b, w_out_bf, l, xf)
    return _rmsnorm(xf, final_norm_g, x.dtype).reshape(batch, seq, d)
```

```python
import functools

import jax
import jax.numpy as jnp
from jax import lax
from jax.experimental import pallas as pl
from jax.experimental.pallas import tpu as pltpu

F32, BF16, I32 = jnp.float32, jnp.bfloat16, jnp.int32

EPS = 1e-6
TOPK_MAX = 256
SQRT_HALF = 0.7071067811865476
LOG2E = 1.4426950408889634

V7X_LANES = 128
V7X_SUBLANES = 8
BF16_SUBLANES = 2 * V7X_SUBLANES
V7X_VMEM_LIMIT_BYTES = 56 * 2**20

ONES_ROWS = 16

INT_MIN = -(2**31)
MASK_NEG = -1e30


def _cparams(*sem):
    return pltpu.CompilerParams(dimension_semantics=sem, vmem_limit_bytes=V7X_VMEM_LIMIT_BYTES)


def _tile(n, want):
    t = min(n, want)
    while n % t:
        t //= 2
    return t


def _rmsnorm_kernel(x_ref, g_ref, o_ref):
    x = x_ref[...]
    ms = jnp.mean(x * x, axis=-1, keepdims=True)
    o_ref[...] = (x * lax.rsqrt(ms + EPS) * g_ref[...]).astype(o_ref.dtype)


def _rmsnorm(x, g, out_dtype):
    m, d = x.shape
    tm = _tile(m, 512)
    return pl.pallas_call(
        _rmsnorm_kernel,
        grid=(m // tm,),
        in_specs=[pl.BlockSpec((tm, d), lambda i: (i, 0)), pl.BlockSpec((1, d), lambda i: (0, 0))],
        out_specs=pl.BlockSpec((tm, d), lambda i: (i, 0)),
        out_shape=jax.ShapeDtypeStruct((m, d), out_dtype),
        compiler_params=_cparams("parallel"),
        name="rmsnorm",
    )(x, g.reshape(1, d))


def _gelu(x):
    return 0.5 * x * (1.0 + lax.erf(x * SQRT_HALF))


_ACTS = {"none": lambda x: x, "silu": jax.nn.silu, "gelu": _gelu}


NT_DIMS = (((1,), (1,)), ((), ()))


def _proj_kernel(h_ref, w_ref, o_ref, *, act):
    acc = lax.dot_general(h_ref[...], w_ref[0], NT_DIMS, preferred_element_type=F32)
    o_ref[...] = _ACTS[act](acc).astype(o_ref.dtype)


def _wrows(layer, row0, rows, k, step=None):
    def index(*ids):
        r = row0 if step is None else row0 + ids[0] * step
        return (layer, pl.multiple_of(r, BF16_SUBLANES), 0)

    assert row0 % BF16_SUBLANES == 0 and (step is None or step % BF16_SUBLANES == 0)
    return pl.BlockSpec((pl.Element(1), pl.Element(rows), pl.Element(k)), index)


def _proj(h, wt, layer, cols, act, out_dtype, name):
    m, k = h.shape
    c0, n = cols
    tm, tn = _tile(m, 1024), _tile(n, 1024)
    return pl.pallas_call(
        functools.partial(_proj_kernel, act=act),
        grid=(n // tn, m // tm),
        in_specs=[pl.BlockSpec((tm, k), lambda j, i: (i, 0)), _wrows(layer, c0, tn, k, step=tn)],
        out_specs=pl.BlockSpec((tm, tn), lambda j, i: (i, j)),
        out_shape=jax.ShapeDtypeStruct((m, n), out_dtype),
        compiler_params=_cparams("parallel", "parallel"),
        name=name,
    )(h, wt)


def _latent_kernel(h_ref, w_ref, g_ref, wuk_ref, wuvt_ref, k_ref, vt_ref, *, kscale):
    c = lax.dot_general(h_ref[...], w_ref[0], NT_DIMS, preferred_element_type=F32)
    ms = jnp.mean(c * c, axis=-1, keepdims=True)
    c = (c * lax.rsqrt(ms + EPS) * g_ref[...]).astype(BF16)
    k_ref[...] = (jnp.dot(c, wuk_ref[...], preferred_element_type=F32) * kscale).astype(k_ref.dtype)
    heads, dh, kb = vt_ref.shape[1], vt_ref.shape[2] - ONES_ROWS, c.shape[0]
    vt = lax.dot_general(wuvt_ref[...], c, NT_DIMS, preferred_element_type=F32).astype(vt_ref.dtype)
    ones_row = (lax.broadcasted_iota(I32, (ONES_ROWS, kb), 0) == 0).astype(vt_ref.dtype)
    for hd in range(heads):
        vt_ref[0, hd, :dh, :] = vt[hd * dh:(hd + 1) * dh, :]
        vt_ref[0, hd, dh:, :] = ones_row


def _latent_keys_values(h, wt, layer, cols, g, wuk, wuvt, kscale, kb):
    m, k = h.shape
    c0, cdim = cols
    heads, dh, _ = wuvt.shape
    assert m % kb == 0
    return pl.pallas_call(
        functools.partial(_latent_kernel, kscale=kscale),
        grid=(m // kb,),
        in_specs=[pl.BlockSpec((kb, k), lambda i: (i, 0)), _wrows(layer, c0, cdim, k),
                  pl.BlockSpec((1, cdim), lambda i: (0, 0)), pl.BlockSpec((cdim, heads * dh), lambda i: (0, 0)),
                  pl.BlockSpec((heads * dh, cdim), lambda i: (0, 0))],
        out_specs=[pl.BlockSpec((kb, heads * dh), lambda i: (i, 0)),
                   pl.BlockSpec((1, heads, dh + ONES_ROWS, kb), lambda i: (i, 0, 0, 0))],
        out_shape=[jax.ShapeDtypeStruct((m, heads * dh), BF16),
                   jax.ShapeDtypeStruct((m // kb, heads, dh + ONES_ROWS, kb), BF16)],
        compiler_params=_cparams("parallel"),
        name="latent_keys_values",
    )(h, wt, g.reshape(1, cdim), wuk, wuvt.reshape(heads * dh, cdim))


def _idx_kernel(h_ref, w_ref, g_ref, b_ref, k_ref, wt_ref, *, wscale):
    di, hi = k_ref.shape[1], wt_ref.shape[0]
    r = lax.dot_general(h_ref[...], w_ref[0], NT_DIMS, preferred_element_type=F32)
    k = r[:, :di]
    mu = jnp.mean(k, axis=-1, keepdims=True)
    kc = k - mu
    var = jnp.mean(kc * kc, axis=-1, keepdims=True)
    k_ref[...] = (kc * lax.rsqrt(var + EPS) * g_ref[...] + b_ref[...]).astype(k_ref.dtype)
    wt_ref[...] = r[:, di:].T[:hi, :] * wscale


def _idx_small(h, wt, layer, kcols, wcols, g, b, wscale):
    m, k = h.shape
    (k0, di), (w0, hi) = kcols, wcols
    assert w0 == k0 + di and hi <= di
    tm = _tile(m, 512)
    return pl.pallas_call(
        functools.partial(_idx_kernel, wscale=wscale),
        grid=(m // tm,),
        in_specs=[pl.BlockSpec((tm, k), lambda i: (i, 0)), _wrows(layer, k0, 2 * di, k),
                  pl.BlockSpec((1, di), lambda i: (0, 0)),
                  pl.BlockSpec((1, di), lambda i: (0, 0))],
        out_specs=[pl.BlockSpec((tm, di), lambda i: (i, 0)), pl.BlockSpec((hi, tm), lambda i: (0, i))],
        out_shape=[jax.ShapeDtypeStruct((m, di), BF16), jax.ShapeDtypeStruct((hi, m), F32)],
        compiler_params=_cparams("parallel"),
        name="idx_small",
    )(h, wt, g.reshape(1, di), b.reshape(1, di))


def _spatial_kernel(u_ref, v_ref, gb_ref, ws_ref, bst_ref, vg_ref, vb_ref, o_ref, *, chunk, groups):
    tm, bw = v_ref.shape
    e = bw // groups
    v = v_ref[...]
    mu = jnp.mean(v, axis=-1, keepdims=True)
    vc = v - mu
    var = jnp.mean(vc * vc, axis=-1, keepdims=True)
    vn = (vc * lax.rsqrt(var + EPS) * vg_ref[...] + vb_ref[...]).astype(BF16)
    row = lax.broadcasted_iota(I32, (chunk, chunk), 0)
    col = lax.broadcasted_iota(I32, (chunk, chunk), 1)
    causal = col <= row
    for g in range(groups):
        wsg = jnp.where(causal, ws_ref[g], 0.0).astype(BF16)
        bias = jnp.broadcast_to(bst_ref[:, g:g + 1], (chunk, e))
        for c in range(tm // chunk):
            rows, cols = slice(c * chunk, (c + 1) * chunk), slice(g * e, (g + 1) * e)
            mixed = jnp.dot(wsg, vn[rows, cols], preferred_element_type=F32) + bias
            o_ref[rows, cols] = (u_ref[rows, cols] * mixed * gb_ref[rows, cols]).astype(o_ref.dtype)


def _spatial(u, v, gate, w_s, b_s, vg, vb):
    m, bw = u.shape
    groups, chunk, _ = w_s.shape
    tm = 2 * chunk if m % (2 * chunk) == 0 else chunk
    row = lambda i: (i, 0)
    return pl.pallas_call(
        functools.partial(_spatial_kernel, chunk=chunk, groups=groups),
        grid=(m // tm,),
        in_specs=[pl.BlockSpec((tm, bw), row), pl.BlockSpec((tm, bw), row), pl.BlockSpec((tm, bw), row),
                  pl.BlockSpec((groups, chunk, chunk), lambda i: (0, 0, 0)),
                  pl.BlockSpec((chunk, groups), lambda i: (0, 0)),
                  pl.BlockSpec((1, bw), lambda i: (0, 0)), pl.BlockSpec((1, bw), lambda i: (0, 0))],
        out_specs=pl.BlockSpec((tm, bw), row),
        out_shape=jax.ShapeDtypeStruct((m, bw), BF16),
        compiler_params=_cparams("parallel"),
        name="spatial_gating",
    )(u, v, gate, w_s, b_s.T, vg.reshape(1, bw), vb.reshape(1, bw))


KB_PER_STEP = 2
SELECT_CHUNK_STEP = 8
WORD_BITS = 32
CHUNK_ROWS = WORD_BITS * V7X_SUBLANES


def _bit_transpose32(a):
    a = list(a)
    m, j = 0x0000FFFF, 16
    while j:
        k = 0
        while k < WORD_BITS:
            t = (a[k] ^ lax.shift_right_logical(a[k + j], jnp.int32(j))) & jnp.int32(m)
            a[k] = a[k] ^ t
            a[k + j] = a[k + j] ^ lax.shift_left(t, jnp.int32(j))
            k = (k + j + 1) & ~j
        j >>= 1
        m = (m ^ (m << j)) & 0xFFFFFFFF
    return a


def _attn_kernel(qi_ref, ki_ref, q_ref, qidx_ref, wt_ref, gate_ref, kidx_ref, kp_ref, vt_ref, o_ref,
                 planes_ref, alive_ref, sel_ref, bias_ref, acc_ref, m_ref,
                 *, qb, kb, topk, seq_bits):
    t = pl.program_id(1)
    i, kg = qi_ref[t], ki_ref[t]
    heads, dh = vt_ref.shape[1], vt_ref.shape[2] - ONES_ROWS
    iheads, di = wt_ref.shape[0], kidx_ref.shape[1]
    cpk = kb // CHUNK_ROWS
    n_kb = ((i + 1) * qb + kb - 1) // kb
    half_tail = n_kb * kb - (i + 1) * qb >= kb // 2
    nt = NT_DIMS

    @pl.when(kg == 0)
    def _select():
        wt = wt_ref[...]
        qpos = i * qb + lax.broadcasted_iota(I32, (1, qb), 1)

        @pl.when(i == 0)
        def _():
            planes_ref[...] = jnp.zeros(planes_ref.shape, I32)

        def score_block(kbj, masked, rows=kb):
            r0 = pl.multiple_of(kbj * kb, kb)
            kblk = kidx_ref[pl.ds(r0, rows), :]
            score = jnp.zeros((rows, qb), F32)
            for h in range(iheads):
                lt = lax.dot_general(kblk, qidx_ref[:, h * di:(h + 1) * di], nt, preferred_element_type=F32)
                score = score + jnp.maximum(lt, 0.0) * wt[h:h + 1, :]
            bits = pltpu.bitcast(score, I32)
            key = bits ^ ((bits >> 31) | INT_MIN)
            key = jnp.where(key == 0x7FFFFFFF, INT_MIN, key)
            if masked:
                kpos = r0 + lax.broadcasted_iota(I32, (rows, 1), 0)
                key = jnp.where(kpos <= qpos, key, 0)
            for cc in range(rows // CHUNK_ROWS):
                rws = [key[cc * CHUNK_ROWS + j * V7X_SUBLANES:cc * CHUNK_ROWS + (j + 1) * V7X_SUBLANES, :]
                       for j in range(WORD_BITS)]
                planes = _bit_transpose32(rws)
                for b in range(WORD_BITS):
                    planes_ref[b, kbj * cpk + cc] = planes[b]

        def score_body(kbj, carry):
            score_block(kbj, False)
            return carry

        lax.fori_loop(0, n_kb - 1, score_body, 0)
        pl.when(half_tail)(functools.partial(score_block, n_kb - 1, True, kb // 2))
        pl.when(jnp.logical_not(half_tail))(functools.partial(score_block, n_kb - 1, True))

        def popsum(words):
            return jnp.sum(jnp.sum(lax.population_count(words), axis=0), axis=0, keepdims=True)

        n_used = ((i + 1) * qb + CHUNK_ROWS - 1) // CHUNK_ROWS

        def select_topk(nc):
            shape = (nc,) + alive_ref.shape[1:]
            cidx = lax.broadcasted_iota(I32, shape, 0)
            alive_ref[:nc] = jnp.where(cidx < n_used, -1, 0)
            sel_ref[:nc] = jnp.zeros(shape, I32)

            def radix_body(b, k_rem):
                cnt = jnp.zeros((1, qb), I32)
                for g0 in range(0, nc, SELECT_CHUNK_STEP):
                    g = slice(g0, min(g0 + SELECT_CHUNK_STEP, nc))
                    cnt = cnt + popsum(alive_ref[g] & planes_ref[b, g])
                take1 = cnt >= k_rem
                for g0 in range(0, nc, SELECT_CHUNK_STEP):
                    g = slice(g0, min(g0 + SELECT_CHUNK_STEP, nc))
                    plane, alive = planes_ref[b, g], alive_ref[g]
                    ones = alive & plane
                    alive_ref[g] = jnp.where(take1, ones, alive & ~plane)
                    sel_ref[g] = jnp.where(take1, sel_ref[g], sel_ref[g] | ones)
                return jnp.where(take1, k_rem, k_rem - cnt)

            need = lax.fori_loop(0, WORD_BITS, radix_body, jnp.minimum(qpos + 1, topk))

            @pl.when(jnp.max(popsum(alive_ref[:nc]) - need) > 0)
            def _():
                ties = alive_ref[:nc]
                row0 = cidx * CHUNK_ROWS + lax.broadcasted_iota(I32, shape, 1)

                def rows_below(lim):
                    nj = jnp.clip(lax.shift_right_arithmetic(lim - row0 + (V7X_SUBLANES - 1), 3), 0, WORD_BITS)
                    return jnp.where(nj <= 0, 0, lax.shift_left(jnp.int32(-1), WORD_BITS - jnp.maximum(nj, 1)))

                def idx_body(b, lim):
                    cand = lim + lax.shift_left(jnp.int32(1), seq_bits - 1 - b)
                    return jnp.where(popsum(ties & rows_below(cand)) < need, cand, lim)

                lim = lax.fori_loop(0, seq_bits, idx_body, jnp.zeros((1, qb), I32))
                alive_ref[:nc] = ties & rows_below(lim + 1)

            sel_ref[:nc] = sel_ref[:nc] | alive_ref[:nc]

        nc_total = alive_ref.shape[0]
        sizes = sorted({min(nc_total, s) for s in range(SELECT_CHUNK_STEP, nc_total + SELECT_CHUNK_STEP, SELECT_CHUNK_STEP)})
        for lo, nc in zip([0] + sizes[:-1], sizes):
            pl.when((n_used > lo) & (n_used <= nc))(functools.partial(select_topk, nc))

        def bias_body(c, carry):
            words = sel_ref[c]
            rows8 = []
            for j in range(WORD_BITS):
                top = words if j == 0 else lax.shift_left(words, jnp.int32(j))
                rows8.append(jnp.where(top < 0, 0.0, MASK_NEG))
            for j in range(0, WORD_BITS, 2):
                r = pl.multiple_of(c * CHUNK_ROWS, CHUNK_ROWS) + j * V7X_SUBLANES
                bias_ref[pl.ds(r, BF16_SUBLANES), :] = jnp.concatenate(rows8[j:j + 2], axis=0).astype(bias_ref.dtype)
            return carry

        lax.fori_loop(0, n_used, bias_body, 0)
        m_ref[...] = jnp.full(m_ref.shape, MASK_NEG, F32)
        acc_ref[...] = jnp.zeros(acc_ref.shape, F32)

    def attend(sub, rows):
        r0 = pl.multiple_of((kg * KB_PER_STEP + sub) * kb, kb)
        rk = pl.multiple_of(sub * kb, kb)
        bias = bias_ref[pl.ds(r0, rows), :].astype(F32)
        s = jnp.concatenate(
            [lax.dot_general(kp_ref[pl.ds(rk, rows), h * dh:(h + 1) * dh], q_ref[:, h * dh:(h + 1) * dh], nt,
                             preferred_element_type=F32) + bias for h in range(heads)], axis=1)
        m_old = m_ref[...]
        m_new = jnp.maximum(m_old, jnp.max(s, axis=0, keepdims=True))
        alpha = jnp.exp2(m_old - m_new)
        p = jnp.exp2(s - m_new).astype(vt_ref.dtype)
        m_ref[...] = m_new
        for h in range(heads):
            qc = slice(h * qb, (h + 1) * qb)
            pv = jnp.dot(vt_ref[sub, h, :, :rows], p[:, qc], preferred_element_type=F32)
            acc_ref[h] = acc_ref[h] * alpha[:, qc] + pv

    def block_body(sub, carry):
        attend(sub, kb)
        return carry

    n_here = jnp.minimum(KB_PER_STEP, n_kb - kg * KB_PER_STEP)
    is_last = (kg + 1) * KB_PER_STEP >= n_kb
    trim = jnp.logical_and(is_last, half_tail)
    lax.fori_loop(0, n_here - trim.astype(I32), block_body, 0)
    pl.when(trim)(lambda: attend(n_here - 1, kb // 2))

    @pl.when((kg + 1) * KB_PER_STEP >= n_kb)
    def _finish():
        for h in range(heads):
            hc = slice(h * dh, (h + 1) * dh)
            o_t = acc_ref[h, :dh, :] * (1.0 / acc_ref[h, dh:dh + 1, :])
            o_ref[:, hc] = (o_t.T * gate_ref[:, hc]).astype(o_ref.dtype)


def _causal_steps(nq, qb, kb):
    qi, ki = [], []
    for i in range(nq):
        n_kb = ((i + 1) * qb + kb - 1) // kb
        for k in range((n_kb + KB_PER_STEP - 1) // KB_PER_STEP):
            qi.append(i)
            ki.append(k)
    return jnp.asarray(qi, I32), jnp.asarray(ki, I32)


def _sparse_attention(q, qidx, wt, gate, kidx, kproj, vt, *, batch, kb):
    m, aw = q.shape
    seq = m // batch
    heads, dh = vt.shape[1], vt.shape[2] - ONES_ROWS
    iheads, di = wt.shape[0], kidx.shape[1]
    qb = min(2 * V7X_LANES, seq)
    nq, nkg = seq // qb, seq // (KB_PER_STEP * kb)
    assert seq % (KB_PER_STEP * kb) == 0 and kb % CHUNK_ROWS == 0
    topk = min(TOPK_MAX, seq // 4)
    qi, ki = _causal_steps(nq, qb, kb)
    qrow = lambda b, t, qi, ki: (b * nq + qi[t], 0)
    nchunks = seq // CHUNK_ROWS
    grid_spec = pltpu.PrefetchScalarGridSpec(
        num_scalar_prefetch=2,
        grid=(batch, qi.shape[0]),
        in_specs=[pl.BlockSpec((qb, aw), qrow),
                  pl.BlockSpec((qb, iheads * di), qrow),
                  pl.BlockSpec((iheads, qb), lambda b, t, qi, ki: (0, b * nq + qi[t])),
                  pl.BlockSpec((qb, aw), qrow),
                  pl.BlockSpec((seq, di), lambda b, t, qi, ki: (b, 0), pipeline_mode=pl.Buffered(1)),
                  pl.BlockSpec((KB_PER_STEP * kb, aw), lambda b, t, qi, ki: (b * nkg + ki[t], 0)),
                  pl.BlockSpec((KB_PER_STEP, heads, dh + ONES_ROWS, kb), lambda b, t, qi, ki: (b * nkg + ki[t], 0, 0, 0))],
        out_specs=pl.BlockSpec((qb, aw), qrow),
        scratch_shapes=[pltpu.VMEM((WORD_BITS, nchunks, V7X_SUBLANES, qb), I32),
                        pltpu.VMEM((nchunks, V7X_SUBLANES, qb), I32),
                        pltpu.VMEM((nchunks, V7X_SUBLANES, qb), I32),
                        pltpu.VMEM((seq, qb), BF16),
                        pltpu.VMEM((heads, dh + ONES_ROWS, qb), F32),
                        pltpu.VMEM((1, heads * qb), F32)])
    return pl.pallas_call(
        functools.partial(_attn_kernel, qb=qb, kb=kb, topk=topk, seq_bits=seq.bit_length()),
        grid_spec=grid_spec,
        out_shape=jax.ShapeDtypeStruct((m, aw), BF16),
        compiler_params=_cparams("parallel", "arbitrary"),
        name="sparse_attention",
    )(qi, ki, q, qidx, wt, gate, kidx, kproj, vt)


def _outproj_kernel(ya_ref, yb_ref, wa_ref, wb_ref, x_ref, o_ref):
    acc = jnp.dot(ya_ref[...], wa_ref[...], preferred_element_type=F32)
    acc = acc + jnp.dot(yb_ref[...], wb_ref[...], preferred_element_type=F32)
    o_ref[...] = x_ref[...] + acc


def _wspec(w, layer, rows, cols, index):
    return pl.BlockSpec((None, rows, cols), lambda *ids: (layer,) + tuple(index(*ids)))


def _outproj(ya, yb, w, layer, x):
    m, ka = ya.shape
    kbw = yb.shape[1]
    n = w.shape[2]
    assert ka == kbw and w.shape[1] == ka + kbw
    tm, tn = _tile(m, 1024), _tile(n, 1024)
    return pl.pallas_call(
        _outproj_kernel,
        grid=(n // tn, m // tm),
        in_specs=[pl.BlockSpec((tm, ka), lambda j, i: (i, 0)), pl.BlockSpec((tm, kbw), lambda j, i: (i, 0)),
                  _wspec(w, layer, ka, tn, lambda j, i: (0, j)), _wspec(w, layer, kbw, tn, lambda j, i: (1, j)),
                  pl.BlockSpec((tm, tn), lambda j, i: (i, j))],
        out_specs=pl.BlockSpec((tm, tn), lambda j, i: (i, j)),
        out_shape=jax.ShapeDtypeStruct((m, n), F32),
        compiler_params=_cparams("parallel", "parallel"),
        name="outproj",
    )(ya, yb, w, w, x)


def kernel(x, norm_g, w_in, kv_norm_g, idx_k_norm_g, idx_k_norm_b, w_uk, w_uv, v_norm_g, v_norm_b, w_s, b_s,
           w_out, final_norm_g):
    batch, seq, d = x.shape
    depth = w_in.shape[0]
    cdim, heads, dh = w_uk.shape[1:]
    aw = heads * dh
    di = idx_k_norm_g.shape[1]
    bw = v_norm_g.shape[1]
    iheads = (w_in.shape[2] - 2 * aw - cdim - di - 3 * bw) // (di + 1)
    kb = _tile(seq, 512)

    names = ("q", "c_kv", "gate_a", "q_idx", "k_idx", "w_idx", "u", "v", "gate_b")
    sizes = (aw, cdim, aw, iheads * di, di, iheads, bw, bw, bw)
    seg, off = {}, 0
    for nm, sz in zip(names, sizes):
        seg[nm] = (off, off + sz)
        off += sz

    w_in_t = jnp.swapaxes(w_in, 1, 2).astype(BF16)
    w_out_bf = w_out.astype(BF16)
    cols = lambda nm: (seg[nm][0], seg[nm][1] - seg[nm][0])

    xf = x.reshape(batch * seq, d)
    for l in range(depth):
        h = _rmsnorm(xf, norm_g[l], BF16)
        q = _proj(h, w_in_t, l, cols("q"), "none", BF16, "proj_q")
        kproj, vt = _latent_keys_values(h, w_in_t, l, cols("c_kv"), kv_norm_g[l],
                                        w_uk[l].reshape(cdim, aw).astype(BF16),
                                        jnp.transpose(w_uv[l], (1, 2, 0)).astype(BF16), (dh ** -0.5) * LOG2E, kb)
        gate_a = _proj(h, w_in_t, l, cols("gate_a"), "silu", F32, "proj_gate_a")
        qidx = _proj(h, w_in_t, l, cols("q_idx"), "none", BF16, "proj_q_idx")
        kidx, wt = _idx_small(h, w_in_t, l, cols("k_idx"), cols("w_idx"), idx_k_norm_g[l], idx_k_norm_b[l],
                              (iheads ** -0.5) * (di ** -0.5))
        y_a = _sparse_attention(q, qidx, wt, gate_a, kidx, kproj, vt, batch=batch, kb=kb)
        u = _proj(h, w_in_t, l, cols("u"), "gelu", F32, "proj_u")
        v = _proj(h, w_in_t, l, cols("v"), "gelu", F32, "proj_v")
        gate_b = _proj(h, w_in_t, l, cols("gate_b"), "silu", F32, "proj_gate_b")
        y_b = _spatial(u, v, gate_b, w_s[l], b_s[l], v_norm_g[l], v_norm_b[l])
        xf = _outproj(y_a, y_b, w_out_bf, l, xf)
    return _rmsnorm(xf, final_norm_g, x.dtype).reshape(batch, seq, d)
```
